```python
import math
import jax, jax.numpy as jnp
from jax import lax
import numpy as np

D_MODEL = 2048
BATCH = 8
SEQ = 2048
DEPTH = 1
DEC_BATCH = 128
DEC_SEQ = 4
PAST_LEN = 2048
PAGE_SIZE = 128

N_HEADS = 16
HEAD_DIM = 128
N_KV_HEADS = 4
GROUP = N_HEADS // N_KV_HEADS
IDX_HEADS = 16
IDX_DIM = 128
TOPK_MAX = 256
ROPE_THETA = 10000.0
Q_BLOCK = 128
POOL_WINDOWS = (2, 4, 8, 16)
N_POOL_GROUPS = 4
POOL_GROUP_DIM = 256
D_POOL = N_POOL_GROUPS * POOL_GROUP_DIM
POOL_OUT_DIM = D_MODEL // N_POOL_GROUPS
POOL_STATE = max(POOL_WINDOWS) - 1
PEER_HEADS = 8
PEER_N_KEYS = 128
PEER_N_EXPERTS = PEER_N_KEYS * PEER_N_KEYS
PEER_D_KEY = 256
PEER_HALF = PEER_D_KEY // 2
PEER_TOPK = 16
TOK_BLOCK = 128
EPS = 1e-6
Q_COLS = N_HEADS * HEAD_DIM
KV_COLS = N_KV_HEADS * HEAD_DIM
IDXQ_COLS = IDX_HEADS * IDX_DIM
W_IN_COLS = Q_COLS + 2 * KV_COLS + IDXQ_COLS + IDX_DIM + IDX_HEADS + D_POOL + 2 * D_MODEL

kernel_name = 'dsa_pool_peer_adaln_hybrid_step'


def split_points():
    widths = (Q_COLS, KV_COLS, KV_COLS, IDXQ_COLS, IDX_DIM, IDX_HEADS, D_POOL, D_MODEL)
    pts, acc = [], 0
    for w in widths:
        acc += w
        pts.append(acc)
    return tuple(pts)


def rms_norm(x, g):
    xf = x.astype(jnp.float32)
    y = xf * lax.rsqrt(jnp.mean(xf * xf, axis=-1, keepdims=True) + EPS)
    return (y * g.astype(jnp.float32)).astype(x.dtype)


def rope(x, pos):
    half = x.shape[-1] // 2
    freqs = ROPE_THETA ** (-jnp.arange(half, dtype=jnp.float32) / half)
    ang = pos.astype(jnp.float32)[:, None] * freqs[None, :]
    cos = jnp.cos(ang)[:, None, :]
    sin = jnp.sin(ang)[:, None, :]
    xf = x.astype(jnp.float32)
    x1, x2 = xf[..., :half], xf[..., half:]
    return jnp.concatenate([x1 * cos - x2 * sin, x1 * sin + x2 * cos], axis=-1).astype(x.dtype)


def modulation(c, w_mod, b_mod):
    m = jax.nn.silu(c) @ w_mod + b_mod
    m = m.reshape(c.shape[0], 6, D_MODEL)[:, :, None, :]
    return m[:, 0], m[:, 1], m[:, 2], m[:, 3], m[:, 4], m[:, 5]


def mixer_inputs(x, shift, scale, pos, norm1_g, w_in, q_norm_g, k_norm_g):
    B, T, _ = x.shape
    h = rms_norm(x, norm1_g) * (1 + scale) + shift
    proj = h @ w_in
    q, k, v, iq, ik, iw, p_in, ga, gb = jnp.split(proj, split_points(), axis=-1)
    q = rope(rms_norm(q.reshape(B, T, N_HEADS, HEAD_DIM), q_norm_g), pos)
    k = rope(rms_norm(k.reshape(B, T, N_KV_HEADS, HEAD_DIM), k_norm_g), pos)
    v = v.reshape(B, T, N_KV_HEADS, HEAD_DIM)
    iq = rope(iq.reshape(B, T, IDX_HEADS, IDX_DIM), pos)
    ik = rope(ik[:, :, None, :], pos)[:, :, 0, :]
    iw = iw * (IDX_HEADS ** -0.5 * IDX_DIM ** -0.5)
    return q, k, v, iq, ik, iw, p_in, ga, gb


def take_rows(rows, sel):
    return jax.vmap(lambda r, i: r[i])(rows, sel)


def sparse_attend(q, iq, iw, pos_q, k_idx_all, gather_kv, k_sel):
    L = k_idx_all.shape[1]
    dots = jax.nn.relu(jnp.einsum('bqhd,bsd->bqhs', iq.astype(jnp.float32), k_idx_all.astype(jnp.float32)))
    score = jnp.einsum('bqh,bqhs->bqs', iw.astype(jnp.float32), dots)
    valid = jnp.arange(L)[None, :] <= pos_q[:, None]
    score = jnp.where(valid[None], score, -jnp.inf)
    _, sel = lax.top_k(score, k_sel)
    ks, vs = gather_kv(sel)
    B, Tq = q.shape[:2]
    qg = q.reshape(B, Tq, N_KV_HEADS, GROUP, HEAD_DIM)
    logits = jnp.einsum('bqhgd,bqshd->bqhgs', qg, ks).astype(jnp.float32) * (HEAD_DIM ** -0.5)
    sel_ok = sel <= pos_q[None, :, None]
    logits = jnp.where(sel_ok[:, :, None, None, :], logits, -jnp.inf)
    probs = jax.nn.softmax(logits, axis=-1).astype(vs.dtype)
    out = jnp.einsum('bqhgs,bqshd->bqhgd', probs, vs)
    return out.reshape(B, Tq, N_HEADS * HEAD_DIM)


def prompt_attention(q, iq, iw, k, v, ik, k_sel):
    B, S = q.shape[:2]
    nb = S // Q_BLOCK

    def blocks(a):
        return jnp.moveaxis(a.reshape((B, nb, Q_BLOCK) + a.shape[2:]), 1, 0)

    def gather(sel):
        return take_rows(k, sel), take_rows(v, sel)

    def body(args):
        qb, iqb, iwb, posb = args
        return sparse_attend(qb, iqb, iwb, posb, ik, gather, k_sel)

    pos_blocks = jnp.arange(S).reshape(nb, Q_BLOCK)
    out = lax.map(body, (blocks(q), blocks(iq), blocks(iw), pos_blocks))
    return jnp.moveaxis(out, 0, 1).reshape(B, S, N_HEADS * HEAD_DIM)


def sample_attention(q, iq, iw, k_new, v_new, ik_new, cache_k, cache_v, cache_idx_k, page_table, pos, k_sel):
    Bd, T = q.shape[:2]
    past = page_table.shape[1] * PAGE_SIZE
    ik_past = cache_idx_k[page_table].reshape(Bd, past, IDX_DIM)
    ik_all = jnp.concatenate([ik_past, ik_new.astype(ik_past.dtype)], axis=1)
    k_pool = cache_k.reshape(-1, N_KV_HEADS, HEAD_DIM)
    v_pool = cache_v.reshape(-1, N_KV_HEADS, HEAD_DIM)

    def gather(sel):
        in_past = (sel < past)[..., None, None]
        lp = jnp.minimum(sel, past - 1)
        phys = jax.vmap(lambda pt, p: pt[p])(page_table, lp // PAGE_SIZE) * PAGE_SIZE + lp % PAGE_SIZE
        new_i = jnp.clip(sel - past, 0, T - 1)
        ks = jnp.where(in_past, k_pool[phys], take_rows(k_new, new_i).astype(k_pool.dtype))
        vs = jnp.where(in_past, v_pool[phys], take_rows(v_new, new_i).astype(v_pool.dtype))
        return ks, vs

    return sparse_attend(q, iq, iw, pos, ik_all, gather, k_sel)


def multiscale_pool(p_in, prefix, pos, w_pool, pool_scale):
    B, T, _ = p_in.shape
    padded = jnp.concatenate([prefix.astype(p_in.dtype), p_in], axis=1)
    cs = jnp.cumsum(padded.astype(jnp.float32), axis=1)
    cs = jnp.concatenate([jnp.zeros((B, 1, D_POOL), jnp.float32), cs], axis=1)
    end = cs[:, POOL_STATE + 1:, :]
    means = []
    for g, w in enumerate(POOL_WINDOWS):
        lo, hi = g * POOL_GROUP_DIM, (g + 1) * POOL_GROUP_DIM
        start = cs[:, POOL_STATE + 1 - w:POOL_STATE + 1 - w + T, lo:hi]
        cnt = jnp.minimum(pos + 1, w).astype(jnp.float32)[None, :, None]
        means.append((end[..., lo:hi] - start) / cnt)
    mean = jnp.stack(means, axis=2)
    pooled = (mean - p_in.reshape(B, T, N_POOL_GROUPS, POOL_GROUP_DIM).astype(jnp.float32)).astype(p_in.dtype)
    out = jnp.einsum('btgc,gcd->btgd', pooled, w_pool).reshape(B, T, D_MODEL) * pool_scale
    return out, padded[:, -POOL_STATE:, :]


def peer(h, w_peer_q, peer_keys, peer_u, peer_v):
    B, T, D = h.shape
    xt = h.reshape(-1, D)
    n = xt.shape[0]
    nb = -(-n // TOK_BLOCK)
    xt = jnp.pad(xt, ((0, nb * TOK_BLOCK - n), (0, 0))).reshape(nb, TOK_BLOCK, D)

    def blk(xb):
        q = (xb @ w_peer_q).reshape(TOK_BLOCK, PEER_HEADS, 2, PEER_HALF)
        s = jnp.einsum('nhpc,hpkc->nhpk', q, peer_keys).astype(jnp.float32)
        s1, i1 = lax.top_k(s[:, :, 0], PEER_TOPK)
        s2, i2 = lax.top_k(s[:, :, 1], PEER_TOPK)
        cand = (s1[..., :, None] + s2[..., None, :]).reshape(TOK_BLOCK, PEER_HEADS, PEER_TOPK * PEER_TOPK)
        cidx = (i1[..., :, None] * PEER_N_KEYS + i2[..., None, :]).reshape(TOK_BLOCK, PEER_HEADS, PEER_TOPK * PEER_TOPK)
        top_s, where = lax.top_k(cand, PEER_TOPK)
        eidx = jnp.take_along_axis(cidx, where, axis=-1)
        g = jax.nn.softmax(top_s, axis=-1)
        act = jax.nn.gelu(jnp.einsum('nhkd,nd->nhk', peer_u[eidx], xb).astype(jnp.float32), approximate=False)
        return jnp.einsum('nhk,nhkd->nd', (g * act).astype(xb.dtype), peer_v[eidx])

    out = lax.map(blk, xt).reshape(-1, D)[:n]
    return out.reshape(B, T, D)


def setup_inputs(seed: int = 0) -> dict:
    key = jax.random.key(seed)
    ks = jax.random.split(key, 24)
    f32 = jnp.float32
    n_pages = PAST_LEN // PAGE_SIZE
    n_used = DEC_BATCH * n_pages
    n_pool = n_used + (n_used + 3) // 4

    def nrm(k, shape, s=1.0):
        return jax.random.normal(k, shape, f32) * s

    page_table = jax.random.permutation(ks[0], n_pool)[:n_used].reshape(DEC_BATCH, n_pages).astype(jnp.int32)
    return {
        'x_prompt': nrm(ks[1], (BATCH, SEQ, D_MODEL)),
        'x_sample': nrm(ks[2], (DEC_BATCH, DEC_SEQ, D_MODEL)),
        'cache_k': nrm(ks[3], (n_pool, PAGE_SIZE, N_KV_HEADS, HEAD_DIM)),
        'cache_v': nrm(ks[4], (n_pool, PAGE_SIZE, N_KV_HEADS, HEAD_DIM)),
        'cache_idx_k': nrm(ks[5], (n_pool, PAGE_SIZE, IDX_DIM)),
        'state_pool': nrm(ks[6], (DEC_BATCH, POOL_STATE, D_POOL)),
        'page_table': page_table,
        'c_prompt': nrm(ks[7], (BATCH, D_MODEL)),
        'c_sample': nrm(ks[8], (DEC_BATCH, D_MODEL)),
        'norm1_g': 1.0 + nrm(ks[9], (D_MODEL,), 0.02),
        'norm2_g': 1.0 + nrm(ks[10], (D_MODEL,), 0.02),
        'w_mod': nrm(ks[11], (D_MODEL, 6 * D_MODEL), D_MODEL ** -0.5),
        'b_mod': nrm(ks[12], (6 * D_MODEL,), 0.02),
        'w_in': nrm(ks[13], (D_MODEL, W_IN_COLS), D_MODEL ** -0.5),
        'q_norm_g': 1.0 + nrm(ks[14], (HEAD_DIM,), 0.02),
        'k_norm_g': 1.0 + nrm(ks[15], (HEAD_DIM,), 0.02),
        'w_attn_proj': nrm(ks[16], (Q_COLS, D_MODEL), Q_COLS ** -0.5),
        'w_pool': nrm(ks[17], (N_POOL_GROUPS, POOL_GROUP_DIM, POOL_OUT_DIM), POOL_GROUP_DIM ** -0.5),
        'pool_scale': 1.0 + nrm(ks[18], (D_MODEL,), 0.02),
        'w_out': nrm(ks[19], (D_MODEL, D_MODEL), D_MODEL ** -0.5),
        'w_peer_q': nrm(ks[20], (D_MODEL, PEER_HEADS * PEER_D_KEY), D_MODEL ** -0.5),
        'peer_keys': nrm(ks[21], (PEER_HEADS, 2, PEER_N_KEYS, PEER_HALF), PEER_HALF ** -0.5),
        'peer_u': nrm(ks[22], (PEER_N_EXPERTS, D_MODEL), D_MODEL ** -0.5),
        'peer_v': nrm(ks[23], (PEER_N_EXPERTS, D_MODEL), PEER_HEADS ** -0.5),
    }


def reference(x_prompt, x_sample, cache_k, cache_v, cache_idx_k, state_pool, page_table, c_prompt, c_sample,
              norm1_g, norm2_g, w_mod, b_mod, w_in, q_norm_g, k_norm_g, w_attn_proj, w_pool, pool_scale,
              w_out, w_peer_q, peer_keys, peer_u, peer_v):
    S = x_prompt.shape[1]
    T = x_sample.shape[1]
    past = page_table.shape[1] * PAGE_SIZE

    def merge(x, attn, pool_out, ga, gb, gate1):
        merged = jax.nn.sigmoid(ga) * (attn @ w_attn_proj) + jax.nn.sigmoid(gb) * pool_out
        return x + gate1 * (merged @ w_out)

    def channel(x, shift2, scale2, gate2):
        h = rms_norm(x, norm2_g) * (1 + scale2) + shift2
        return x + gate2 * peer(h, w_peer_q, peer_keys, peer_u, peer_v)

    sh1, sc1, g1, sh2, sc2, g2 = modulation(c_prompt, w_mod, b_mod)
    pos_p = jnp.arange(S)
    q, k_p, v_p, iq, ik_p, iw, p_in, ga, gb = mixer_inputs(x_prompt, sh1, sc1, pos_p, norm1_g, w_in, q_norm_g, k_norm_g)
    attn_p = prompt_attention(q, iq, iw, k_p, v_p, ik_p, min(TOPK_MAX, S // 4))
    prefix_p = jnp.zeros((x_prompt.shape[0], POOL_STATE, D_POOL), p_in.dtype)
    pool_p, pool_state_p = multiscale_pool(p_in, prefix_p, pos_p, w_pool, pool_scale)
    h_p = merge(x_prompt, attn_p, pool_p, ga, gb, g1)
    y_prompt = channel(h_p, sh2, sc2, g2)

    sh1, sc1, g1, sh2, sc2, g2 = modulation(c_sample, w_mod, b_mod)
    pos_s = past + jnp.arange(T)
    q, k_s, v_s, iq, ik_s, iw, p_in, ga, gb = mixer_inputs(x_sample, sh1, sc1, pos_s, norm1_g, w_in, q_norm_g, k_norm_g)
    attn_s = sample_attention(q, iq, iw, k_s, v_s, ik_s, cache_k, cache_v, cache_idx_k, page_table, pos_s,
                              min(TOPK_MAX, (past + T) // 4))
    pool_s, pool_state_s = multiscale_pool(p_in, state_pool, pos_s, w_pool, pool_scale)
    h_s = merge(x_sample, attn_s, pool_s, ga, gb, g1)
    y_sample = channel(h_s, sh2, sc2, g2)

    return (y_prompt, y_sample, k_p, v_p, ik_p, pool_state_p, k_s, v_s, ik_s, pool_state_s)
```

```python
import functools
import math

import jax
import jax.numpy as jnp
from jax import lax
from jax.experimental import pallas as pl
from jax.experimental.pallas import tpu as pltpu

F32 = jnp.float32
BF16 = jnp.bfloat16
I32 = jnp.int32

LANES = 128
EPS = 1e-6
ROPE_THETA = 10000.0
TOPK_MAX = 256
POOL_WINDOWS = (2, 4, 8, 16)
POOL_HALO = 16
PEER_TOPK = 16
VMEM_LIMIT = 56 * 1024 * 1024

NT_DIMS = (((1,), (1,)), ((), ()))
TN_DIMS = (((0,), (0,)), ((), ()))
NEG_BIG = -1e30
INT_MIN = -2 ** 31


def _cparams(*sem):
    return pltpu.CompilerParams(dimension_semantics=sem, vmem_limit_bytes=VMEM_LIMIT)


def _mod_kernel(c_ref, w_ref, b_ref, o_ref):
    c = c_ref[...]
    a = c * jax.nn.sigmoid(c)
    o_ref[...] = jnp.dot(a, w_ref[...], preferred_element_type=F32,
                         precision=lax.Precision.HIGHEST) + b_ref[...]


def modulation(c, w_mod, b_mod, tn=1024):
    n, d = c.shape
    cols = w_mod.shape[1]
    return pl.pallas_call(
        _mod_kernel,
        grid=(cols // tn,),
        in_specs=[pl.BlockSpec((n, d), lambda j: (0, 0)),
                  pl.BlockSpec((d, tn), lambda j: (0, j)),
                  pl.BlockSpec((1, tn), lambda j: (0, j))],
        out_specs=pl.BlockSpec((n, tn), lambda j: (0, j)),
        out_shape=jax.ShapeDtypeStruct((n, cols), F32),
        compiler_params=_cparams("arbitrary"),
        name="modulation",
    )(c, w_mod, b_mod.reshape(1, cols))


def _norm_mod_kernel(x_ref, g_ref, sc_ref, sh_ref, o_ref):
    x = x_ref[...]
    y = x * lax.rsqrt(jnp.mean(x * x, axis=-1, keepdims=True) + EPS) * g_ref[...]
    o_ref[...] = (y * (1.0 + sc_ref[...]) + sh_ref[...]).astype(o_ref.dtype)


def _mod_spec(arr, tm, rows_per_group):
    if arr.ndim == 3:
        bpg = rows_per_group // tm
        return pl.BlockSpec((None, 1, arr.shape[-1]), lambda i, *_: (i // bpg, 0, 0))
    return pl.BlockSpec((tm, arr.shape[-1]), lambda i, *_: (i, 0))


def norm_modulate(x, g, scale, shift, rows_per_group, tm):
    n, d = x.shape
    return pl.pallas_call(
        _norm_mod_kernel,
        grid=(n // tm,),
        in_specs=[pl.BlockSpec((tm, d), lambda i: (i, 0)),
                  pl.BlockSpec((1, d), lambda i: (0, 0)),
                  _mod_spec(scale, tm, rows_per_group),
                  _mod_spec(shift, tm, rows_per_group)],
        out_specs=pl.BlockSpec((tm, d), lambda i: (i, 0)),
        out_shape=jax.ShapeDtypeStruct((n, d), BF16),
        compiler_params=_cparams("arbitrary"),
        name="norm_modulate",
    )(x, g.reshape(1, d), scale, shift)


def _proj_kernel(*refs, norm, rope, act, scale, outs):
    h_ref, w_ref = refs[0], refs[1]
    pos = 2
    g_ref = cos_ref = sin_ref = None
    if norm:
        g_ref = refs[pos]
        pos += 1
    if rope:
        cos_ref, sin_ref = refs[pos], refs[pos + 1]
        pos += 2
    out_refs = refs[pos:]
    acc = jnp.dot(h_ref[...], w_ref[...], preferred_element_type=F32)
    tn = acc.shape[1]
    if norm or rope:
        slabs = []
        for u in range(tn // LANES):
            t = acc[:, u * LANES:(u + 1) * LANES]
            if norm:
                t = t * lax.rsqrt(jnp.mean(t * t, axis=-1, keepdims=True) + EPS) * g_ref[...]
            if rope:
                t = t * cos_ref[...] + pltpu.roll(t, LANES // 2, 1) * sin_ref[...]
            slabs.append(t)
    else:
        if scale != 1.0:
            acc = acc * scale
        if act == "sigmoid":
            acc = jax.nn.sigmoid(acc)
        slabs = None
    for kind, o_ref in zip(outs, out_refs):
        if kind == "head_major":
            for u in range(tn // LANES):
                t = acc[:, u * LANES:(u + 1) * LANES] if slabs is None else slabs[u]
                o_ref[u] = t.astype(o_ref.dtype)
        elif slabs is not None:
            for u, t in enumerate(slabs):
                o_ref[:, u * LANES:(u + 1) * LANES] = t.astype(o_ref.dtype)
        else:
            o_ref[...] = acc.astype(o_ref.dtype)


def project(h, w, col0, ncols, *, outs, norm_g=None, cos=None, sin=None, act="none", scale=1.0,
            tm=1024, tn=512):
    n, k = h.shape
    tn = min(tn, ncols)
    tm = min(tm, n)
    assert col0 % tn == 0 and ncols % tn == 0 and n % tm == 0
    c0 = col0 // tn
    in_specs = [pl.BlockSpec((tm, k), lambda i, j: (i, 0)),
                pl.BlockSpec((k, tn), lambda i, j: (0, j + c0))]
    args = [h, w]
    if norm_g is not None:
        in_specs.append(pl.BlockSpec((1, LANES), lambda i, j: (0, 0)))
        args.append(norm_g.reshape(1, LANES))
    if cos is not None:
        in_specs += [pl.BlockSpec((tm, LANES), lambda i, j: (i, 0))] * 2
        args += [cos, sin]
    out_specs, out_shapes = [], []
    for kind, dt in outs:
        if kind == "head_major":
            out_specs.append(pl.BlockSpec((tn // LANES, tm, LANES), lambda i, j: (j, i, 0)))
            out_shapes.append(jax.ShapeDtypeStruct((ncols // LANES, n, LANES), dt))
        else:
            out_specs.append(pl.BlockSpec((tm, tn), lambda i, j: (i, j)))
            out_shapes.append(jax.ShapeDtypeStruct((n, ncols), dt))
    res = pl.pallas_call(
        functools.partial(_proj_kernel, norm=norm_g is not None, rope=cos is not None, act=act,
                          scale=scale, outs=tuple(kd for kd, _ in outs)),
        grid=(n // tm, ncols // tn),
        in_specs=in_specs,
        out_specs=out_specs,
        out_shape=out_shapes,
        compiler_params=_cparams("arbitrary", "arbitrary"),
        name="project",
    )(*args)
    return res


def _sortable_key(score):
    bits = pltpu.bitcast(score, I32)
    return jnp.where(bits < 0, bits ^ jnp.int32(0x7FFFFFFF), bits)


NEG_INF_KEY = (-8388608) ^ 0x7FFFFFFF


def _topk_threshold(keys_ref, nchunks, ck, k, width, max_rows):
    def count(pred):
        def body(c, cnt):
            r0 = pl.multiple_of(c * ck, 8)
            kk = keys_ref[pl.ds(r0, ck), :]
            rows = r0 + lax.broadcasted_iota(I32, kk.shape, 0)
            return cnt + jnp.sum(jnp.where(pred(kk, rows), 1.0, 0.0), axis=0, keepdims=True)
        return lax.fori_loop(0, nchunks, body, jnp.zeros((1, width), F32))

    def bit_step(i, ut):
        uc = ut | jnp.left_shift(jnp.int32(1), jnp.int32(31) - i)
        cand = uc ^ jnp.int32(INT_MIN)
        return jnp.where(count(lambda kk, _: kk >= cand) >= float(k), uc, ut)
    thr = lax.fori_loop(0, 32, bit_step, jnp.zeros((1, width), I32)) ^ jnp.int32(INT_MIN)

    real = thr > jnp.int32(NEG_INF_KEY)
    surplus = jnp.where(real, count(lambda kk, _: kk >= thr) - float(k), 0.0)

    @pl.when(jnp.max(surplus) > 0.0)
    def _():
        need = float(k) - count(lambda kk, _: kk > thr)
        n_bits = max(1, (max_rows - 1).bit_length())

        def idx_step(i, jv):
            cand = jv | jnp.left_shift(jnp.int32(1), jnp.int32(n_bits - 1) - i)
            below = count(lambda kk, rows: jnp.where(kk == thr, rows, jnp.int32(2 ** 30)) < cand)
            return jnp.where(below < need, cand, jv)
        last = lax.fori_loop(0, n_bits, idx_step, jnp.zeros((1, width), I32))
        last = jnp.where(surplus > 0.0, last, jnp.int32(2 ** 30))

        def demote(c, carry):
            r0 = pl.multiple_of(c * ck, 8)
            kk = keys_ref[pl.ds(r0, ck), :]
            rows = r0 + lax.broadcasted_iota(I32, kk.shape, 0)
            keys_ref[pl.ds(r0, ck), :] = jnp.where(jnp.where(kk == thr, rows, 0) > last, thr - 1, kk)
            return carry
        lax.fori_loop(0, nchunks, demote, 0)

    return jnp.maximum(thr, jnp.int32(NEG_INF_KEY + 1))


def _attn_prompt_kernel(iq_ref, iw_ref, ik_ref, q_ref, k_ref, v_ref, o_ref, keys_ref, *,
                        nq, ck, k_sel, n_idx_heads, n_kv, group, sm_scale):
    j = pl.program_id(1)
    nchunks = (j * nq) // ck + 1
    hd = LANES

    qpos = j * nq + lax.broadcasted_iota(I32, (ck, nq), 1)

    def score_chunk(c, carry):
        r0 = pl.multiple_of(c * ck, ck)
        ikc = ik_ref[pl.ds(r0, ck), :]
        acc = jnp.zeros((ck, nq), F32)
        for hp in range(n_idx_heads // 2):
            rhs = iq_ref[2 * hp:2 * hp + 2].reshape(2 * nq, hd)
            d = lax.dot_general(ikc, rhs, NT_DIMS, preferred_element_type=F32)
            for u in range(2):
                h = 2 * hp + u
                acc = acc + jnp.maximum(d[:, u * nq:(u + 1) * nq], 0.0) * iw_ref[h:h + 1, :]
        kpos = r0 + lax.broadcasted_iota(I32, (ck, nq), 0)
        score = jnp.where(kpos <= qpos, acc, -jnp.inf)
        keys_ref[pl.ds(r0, ck), :] = _sortable_key(score)
        return carry

    lax.fori_loop(0, nchunks, score_chunk, 0)

    thr = _topk_threshold(keys_ref, nchunks, ck, k_sel, nq, keys_ref.shape[0])
    thr_g = jnp.concatenate([thr] * group, axis=1)

    gw = group * nq
    for kvh in range(n_kv):
        qg = q_ref[group * kvh:group * (kvh + 1)].reshape(gw, hd)

        def attend(c, carry):
            m, l, acc = carry
            r0 = pl.multiple_of(c * ck, ck)
            kc = k_ref[pl.ds(r0, ck), kvh * hd:(kvh + 1) * hd]
            s = lax.dot_general(kc, qg, NT_DIMS, preferred_element_type=F32) * sm_scale
            kk = keys_ref[pl.ds(r0, ck), :]
            sel = jnp.concatenate([kk] * group, axis=1) >= thr_g
            s = jnp.where(sel, s, NEG_BIG)
            m_new = jnp.maximum(m, jnp.max(s, axis=0, keepdims=True))
            alpha = jnp.exp(m - m_new)
            p = jnp.where(sel, jnp.exp(s - m_new), 0.0)
            l = alpha * l + jnp.sum(p, axis=0, keepdims=True)
            vc = v_ref[pl.ds(r0, ck), kvh * hd:(kvh + 1) * hd]
            pv = lax.dot_general(vc, p.astype(BF16), TN_DIMS, preferred_element_type=F32)
            return m_new, l, alpha * acc + pv

        init = (jnp.full((1, gw), NEG_BIG, F32), jnp.zeros((1, gw), F32), jnp.zeros((hd, gw), F32))
        _, l, acc = lax.fori_loop(0, nchunks, attend, init)
        out_t = acc / l
        for g in range(group):
            head = group * kvh + g
            o_ref[:, head * hd:(head + 1) * hd] = out_t[:, g * nq:(g + 1) * nq].T.astype(o_ref.dtype)


def prompt_attention(iq_hm, iw_t, ik, q_hm, k, v, batch, seq, k_sel, nq=128, ck=512):
    n_idx_heads = iq_hm.shape[0]
    n_heads = q_hm.shape[0]
    hd = LANES
    n_kv = k.shape[1] // hd
    group = n_heads // n_kv
    ck = min(ck, seq)
    nqb = seq // nq
    kern = functools.partial(_attn_prompt_kernel, nq=nq, ck=ck, k_sel=k_sel, n_idx_heads=n_idx_heads,
                             n_kv=n_kv, group=group, sm_scale=hd ** -0.5)
    return pl.pallas_call(
        kern,
        grid=(batch, nqb),
        in_specs=[pl.BlockSpec((n_idx_heads, nq, hd), lambda b, j: (0, b * nqb + j, 0)),
                  pl.BlockSpec((n_idx_heads, nq), lambda b, j: (0, b * nqb + j)),
                  pl.BlockSpec((seq, hd), lambda b, j: (b, 0)),
                  pl.BlockSpec((n_heads, nq, hd), lambda b, j: (0, b * nqb + j, 0)),
                  pl.BlockSpec((seq, n_kv * hd), lambda b, j: (b, 0)),
                  pl.BlockSpec((seq, n_kv * hd), lambda b, j: (b, 0))],
        out_specs=pl.BlockSpec((nq, n_heads * hd), lambda b, j: (b * nqb + j, 0)),
        out_shape=jax.ShapeDtypeStruct((batch * seq, n_heads * hd), BF16),
        scratch_shapes=[pltpu.VMEM((seq, nq), I32)],
        compiler_params=_cparams("arbitrary", "arbitrary"),
        name="prompt_attention",
    )(iq_hm, iw_t, ik, q_hm, k, v)


def _lane_group_sum(x, period):
    shift = period
    while shift < LANES:
        x = x + pltpu.roll(x, shift, 1)
        shift *= 2
    return x


def _attn_sample_kernel(pt_ref, iq_ref, iw_ref, q_ref, ikn_ref, kn_ref, vn_ref, *rest,
                        n_pages, page, t_new, k_sel, sm_scale):
    del pt_ref
    ik_pages = rest[:n_pages]
    k_pages = rest[n_pages:2 * n_pages]
    v_pages = rest[2 * n_pages:3 * n_pages]
    o_ref = rest[3 * n_pages]
    keys_ref = rest[3 * n_pages + 1]
    s_ref = rest[3 * n_pages + 2]
    past = n_pages * page
    new_rows = ikn_ref.shape[0]
    nrows = past + new_rows

    iq = iq_ref[...]
    iw = iw_ref[...]

    def scores(ik_rows):
        d = lax.dot_general(ik_rows.astype(BF16), iq, NT_DIMS, preferred_element_type=F32)
        return _lane_group_sum(jnp.maximum(d, 0.0) * iw, t_new)

    for p in range(n_pages):
        keys_ref[p * page:(p + 1) * page, :] = _sortable_key(scores(ik_pages[p][...]))
    sc_new = scores(ikn_ref[...])
    row = lax.broadcasted_iota(I32, (new_rows, LANES), 0)
    lane_q = lax.broadcasted_iota(I32, (new_rows, LANES), 1) % t_new
    sc_new = jnp.where(row <= lane_q, sc_new, -jnp.inf)
    keys_ref[past:nrows, :] = _sortable_key(sc_new)

    thr = _topk_threshold(keys_ref, 1, nrows, k_sel, LANES, nrows)

    qb = q_ref[...]
    kvw = qb.shape[1]
    m = jnp.full((1, LANES), NEG_BIG, F32)
    blocks = [(k_pages[p], v_pages[p], p * page, page) for p in range(n_pages)]
    blocks.append((kn_ref, vn_ref, past, new_rows))
    for k_blk, _, r0, rows in blocks:
        s = lax.dot_general(k_blk[...].astype(BF16), qb, NT_DIMS, preferred_element_type=F32) * sm_scale
        s = jnp.where(keys_ref[r0:r0 + rows, :] >= thr, s, NEG_BIG)
        s_ref[r0:r0 + rows, :] = s
        m = jnp.maximum(m, jnp.max(s, axis=0, keepdims=True))
    acc = jnp.zeros((LANES, kvw), F32)
    lsum = jnp.zeros((LANES, LANES), F32)
    for _, v_blk, r0, rows in blocks:
        s = s_ref[r0:r0 + rows, :]
        pr = jnp.where(s > 0.5 * NEG_BIG, jnp.exp(s - m), 0.0).astype(BF16)
        acc = acc + lax.dot_general(pr, v_blk[...].astype(BF16), TN_DIMS, preferred_element_type=F32)
        lsum = lsum + lax.dot_general(pr, jnp.ones((rows, LANES), BF16), TN_DIMS, preferred_element_type=F32)
    o_ref[...] = acc / jnp.concatenate([lsum] * (kvw // LANES), axis=1)


def sample_attention(iq_rows, iw_row, q_blk, ik_new, k_new, v_new, cache_idx_k, cache_k2, cache_v2,
                     page_table, t_new, k_sel):
    bd, n_pages = page_table.shape
    page = cache_idx_k.shape[1]
    idx_dim = cache_idx_k.shape[2]
    kvw = cache_k2.shape[2]
    qw = q_blk.shape[1]
    assert qw == LANES
    new_rows = ik_new.shape[1]
    nrows = n_pages * page + new_rows

    def page_spec(width, p):
        return pl.BlockSpec((None, page, width), lambda b, pt: (pt[b * n_pages + p], 0, 0))

    def per_seq(shape):
        return pl.BlockSpec((None,) + shape, lambda b, pt: (b, 0, 0))

    in_specs = [per_seq((LANES, idx_dim)), per_seq((1, LANES)), per_seq((qw, kvw)),
                per_seq((new_rows, idx_dim)), per_seq((new_rows, kvw)), per_seq((new_rows, kvw))]
    in_specs += [page_spec(idx_dim, p) for p in range(n_pages)]
    in_specs += [page_spec(kvw, p) for p in range(n_pages)]
    in_specs += [page_spec(kvw, p) for p in range(n_pages)]
    kern = functools.partial(_attn_sample_kernel, n_pages=n_pages, page=page, t_new=t_new, k_sel=k_sel,
                             sm_scale=LANES ** -0.5)
    return pl.pallas_call(
        kern,
        grid_spec=pltpu.PrefetchScalarGridSpec(
            num_scalar_prefetch=1,
            grid=(bd,),
            in_specs=in_specs,
            out_specs=pl.BlockSpec((None, qw, kvw), lambda b, pt: (b, 0, 0)),
            scratch_shapes=[pltpu.VMEM((nrows, LANES), I32), pltpu.VMEM((nrows, LANES), F32)]),
        out_shape=jax.ShapeDtypeStruct((bd, qw, kvw), F32),
        compiler_params=_cparams("arbitrary"),
        name="sample_attention",
    )(page_table.reshape(-1), iq_rows, iw_row, q_blk, ik_new, k_new, v_new,
      *([cache_idx_k] * n_pages), *([cache_k2] * n_pages), *([cache_v2] * n_pages))


def _pool_prompt_kernel(p_ref, halo_ref, w_ref, ps_ref, o_ref, ext_ref, *, tm):
    sblk = pl.program_id(1)
    c = p_ref.shape[-1]
    gc = c // len(POOL_WINDOWS)
    od = o_ref.shape[-1] // len(POOL_WINDOWS)
    ext_ref[0:POOL_HALO, :] = jnp.where(sblk > 0, halo_ref[...], 0.0)
    ext_ref[POOL_HALO:POOL_HALO + tm, :] = p_ref[...]
    pos = sblk * tm + lax.broadcasted_iota(I32, (tm, gc), 0)
    for g, w in enumerate(POOL_WINDOWS):
        cols = slice(g * gc, (g + 1) * gc)
        tot = ext_ref[POOL_HALO:POOL_HALO + tm, cols]
        for dlt in range(1, w):
            tot = tot + ext_ref[POOL_HALO - dlt:POOL_HALO - dlt + tm, cols]
        cnt = jnp.minimum(pos + 1, w).astype(F32)
        pooled = tot / cnt - p_ref[:, cols]
        out = jnp.dot(pooled.astype(BF16), w_ref[g], preferred_element_type=F32)
        o_ref[:, g * od:(g + 1) * od] = (out * ps_ref[:, g * od:(g + 1) * od]).astype(o_ref.dtype)


def pool_prompt(p_in, w_pool, pool_scale, tm=512):
    b, s, c = p_in.shape
    g, gc, od = w_pool.shape
    tm = min(tm, s)
    nsb = s // tm
    hb = tm // POOL_HALO
    return pl.pallas_call(
        functools.partial(_pool_prompt_kernel, tm=tm),
        grid=(b, nsb),
        in_specs=[pl.BlockSpec((None, tm, c), lambda bi, si: (bi, si, 0)),
                  pl.BlockSpec((None, POOL_HALO, c), lambda bi, si: (bi, jnp.maximum(si * hb - 1, 0), 0)),
                  pl.BlockSpec((g, gc, od), lambda bi, si: (0, 0, 0)),
                  pl.BlockSpec((1, g * od), lambda bi, si: (0, 0))],
        out_specs=pl.BlockSpec((tm, g * od), lambda bi, si: (bi * nsb + si, 0)),
        out_shape=jax.ShapeDtypeStruct((b * s, g * od), F32),
        scratch_shapes=[pltpu.VMEM((POOL_HALO + tm, c), F32)],
        compiler_params=_cparams("arbitrary", "arbitrary"),
        name="pool_prompt",
    )(p_in, p_in, w_pool, pool_scale.reshape(1, g * od))


def _pool_sample_kernel(pad_ref, w_ref, ps_ref, o_ref, *, t_new, past):
    n_state = pad_ref.shape[0] - t_new
    c = pad_ref.shape[-1]
    gc = c // len(POOL_WINDOWS)
    od = o_ref.shape[-1] // len(POOL_WINDOWS)
    for t in range(t_new):
        for g, w in enumerate(POOL_WINDOWS):
            cols = slice(g * gc, (g + 1) * gc)
            cur = pad_ref[n_state + t, :, cols]
            tot = cur
            for dlt in range(1, w):
                tot = tot + pad_ref[n_state + t - dlt, :, cols]
            pooled = tot / float(min(past + t + 1, w)) - cur
            out = jnp.dot(pooled.astype(BF16), w_ref[g], preferred_element_type=F32)
            o_ref[t, :, g * od:(g + 1) * od] = out * ps_ref[:, g * od:(g + 1) * od]


def pool_sample(pad_t, w_pool, pool_scale, t_new, past):
    _, bd, c = pad_t.shape
    g, gc, od = w_pool.shape
    return pl.pallas_call(
        functools.partial(_pool_sample_kernel, t_new=t_new, past=past),
        out_shape=jax.ShapeDtypeStruct((t_new, bd, g * od), F32),
        compiler_params=pltpu.CompilerParams(vmem_limit_bytes=VMEM_LIMIT),
        name="pool_sample",
    )(pad_t, w_pool, pool_scale.reshape(1, g * od))


def _merge_kernel(a_ref, w_ref, ga_ref, gb_ref, pool_ref, o_ref):
    acc = jnp.dot(a_ref[...], w_ref[...], preferred_element_type=F32)
    o_ref[...] = (ga_ref[...].astype(F32) * acc + gb_ref[...].astype(F32) * pool_ref[...]).astype(o_ref.dtype)


def merge_branches(attn, w_attn_proj, sig_a, sig_b, pool_out, tm=1024, tn=512):
    n, k = attn.shape
    d = w_attn_proj.shape[1]
    tm, tn = min(tm, n), min(tn, d)
    blk = pl.BlockSpec((tm, tn), lambda i, j: (i, j))
    return pl.pallas_call(
        _merge_kernel,
        grid=(n // tm, d // tn),
        in_specs=[pl.BlockSpec((tm, k), lambda i, j: (i, 0)),
                  pl.BlockSpec((k, tn), lambda i, j: (0, j)), blk, blk, blk],
        out_specs=blk,
        out_shape=jax.ShapeDtypeStruct((n, d), BF16),
        compiler_params=_cparams("arbitrary", "arbitrary"),
        name="merge_branches",
    )(attn, w_attn_proj, sig_a, sig_b, pool_out)


def _out_proj_kernel(m_ref, w_ref, x_ref, gate_ref, o_ref):
    acc = jnp.dot(m_ref[...], w_ref[...], preferred_element_type=F32)
    o_ref[...] = x_ref[...] + gate_ref[...] * acc


def out_project(merged, w_out, x, gate, rows_per_group, tm=1024, tn=512):
    n, k = merged.shape
    d = w_out.shape[1]
    tm, tn = min(tm, n), min(tn, d)
    if gate.ndim == 3:
        tm = min(tm, rows_per_group)
        bpg = rows_per_group // tm
        gate_spec = pl.BlockSpec((None, 1, tn), lambda i, j: (i // bpg, 0, j))
    else:
        gate_spec = pl.BlockSpec((tm, tn), lambda i, j: (i, j))
    blk = pl.BlockSpec((tm, tn), lambda i, j: (i, j))
    return pl.pallas_call(
        _out_proj_kernel,
        grid=(n // tm, d // tn),
        in_specs=[pl.BlockSpec((tm, k), lambda i, j: (i, 0)),
                  pl.BlockSpec((k, tn), lambda i, j: (0, j)), blk, gate_spec],
        out_specs=blk,
        out_shape=jax.ShapeDtypeStruct((n, d), F32),
        compiler_params=_cparams("arbitrary", "arbitrary"),
        name="out_project",
    )(merged, w_out, x, gate)


def _top_lists(s, n_top):
    rows, t = s.shape
    iota = lax.broadcasted_iota(I32, (rows, t), 0).astype(F32)
    rank = jnp.full((rows, t), float(rows), F32)
    work = s
    vals = []
    for r in range(n_top):
        m = jnp.max(work, axis=0, keepdims=True)
        idx = jnp.min(jnp.where(work == m, iota, float(rows)), axis=0, keepdims=True)
        hit = iota == idx
        work = jnp.where(hit, -jnp.inf, work)
        rank = jnp.where(hit, float(r), rank)
        vals.append(m)
    return jnp.concatenate(vals, axis=0), rank


def _peer_route_kernel(pq_ref, keys_ref, r2_ref, e2_ref, n1_ref, e1_ref, *, n_heads, n_top):
    t = pq_ref.shape[1]
    pieces = []
    for a in range(n_top):
        nb = n_top // (a + 1)
        rows = -(-nb // 8) * 8
        pieces.append((a, nb, rows))
    flat_idx = jnp.concatenate(
        [a * n_top + lax.broadcasted_iota(I32, (rows, t), 0) for a, _, rows in pieces], axis=0).astype(F32)
    n_cand = flat_idx.shape[0]

    def route_head(h, carry):
        s1 = lax.dot_general(keys_ref[2 * h], pq_ref[2 * h], NT_DIMS, preferred_element_type=F32)
        s2 = lax.dot_general(keys_ref[2 * h + 1], pq_ref[2 * h + 1], NT_DIMS, preferred_element_type=F32)
        v1, rank1 = _top_lists(s1, n_top)
        v2, rank2 = _top_lists(s2, n_top)
        cand = jnp.concatenate(
            [jnp.where(lax.broadcasted_iota(I32, (rows, t), 0) < nb, v1[a:a + 1] + v2[:rows], -jnp.inf)
             for a, nb, rows in pieces], axis=0)
        chosen = jnp.zeros((n_cand, t), F32)
        top = cand[0:1]
        z = jnp.zeros((1, t), F32)
        for _ in range(n_top):
            m = jnp.max(cand, axis=0, keepdims=True)
            idx = jnp.min(jnp.where(cand == m, flat_idx, float(n_top * n_top)), axis=0, keepdims=True)
            hit = flat_idx == idx
            cand = jnp.where(hit, -jnp.inf, cand)
            chosen = jnp.where(hit, 1.0, chosen)
            z = z + jnp.exp(m - top)
        n1 = jnp.zeros_like(rank1)
        r0 = 0
        for a, _, rows in pieces:
            n_a = jnp.sum(chosen[r0:r0 + rows], axis=0, keepdims=True)
            n1 = jnp.where(rank1 == float(a), n_a, n1)
            r0 += rows
        inv_z = 1.0 / z
        r2_ref[h] = rank2
        e2_ref[h] = jnp.where(rank2 < float(n_top), jnp.exp(s2 - v2[0:1]), 0.0)
        n1_ref[h] = n1
        e1_ref[h] = jnp.where(rank1 < float(n_top), jnp.exp(s1 - v1[0:1]), 0.0) * inv_z
        return carry

    lax.fori_loop(0, n_heads, route_head, 0)


def peer_route(pq_hm, keys_flat, n_heads, tp=256):
    n = pq_hm.shape[1]
    nk, half = keys_flat.shape[1], keys_flat.shape[2]
    tp = min(tp, n)
    out_spec = pl.BlockSpec((n_heads, nk, tp), lambda i: (0, 0, i))
    out_shape = jax.ShapeDtypeStruct((n_heads, nk, n), F32)
    return pl.pallas_call(
        functools.partial(_peer_route_kernel, n_heads=n_heads, n_top=PEER_TOPK),
        grid=(n // tp,),
        in_specs=[pl.BlockSpec((2 * n_heads, tp, half), lambda i: (0, i, 0)),
                  pl.BlockSpec(keys_flat.shape, lambda i: (0, 0, 0))],
        out_specs=[out_spec] * 4,
        out_shape=[out_shape] * 4,
        compiler_params=_cparams("arbitrary"),
        name="peer_route",
    )(pq_hm, keys_flat)


def _peer_expert_kernel(h_ref, u_ref, vt_ref, r2_ref, e2_ref, n1_ref, e1_ref, o_ref,
                        a_ref, w_ref, acc_ref, *, n_heads, nk):
    e = pl.program_id(1)
    eb, tm = a_ref.shape

    @pl.when(e == 0)
    def _():
        acc_ref[...] = jnp.zeros_like(acc_ref)

    a_ref[...] = lax.dot_general(u_ref[...], h_ref[...], NT_DIMS, preferred_element_type=F32)

    def gate_rows(s, carry):
        r0 = pl.multiple_of(s * nk, nk)
        for lc in range(tm // LANES):
            cols = slice(lc * LANES, (lc + 1) * LANES)
            gsum = jnp.zeros((nk, LANES), F32)
            for h in range(n_heads):
                keep = r2_ref[h, :, cols] < n1_ref[h, s, :, cols]
                gsum = gsum + jnp.where(keep, e2_ref[h, :, cols], 0.0) * e1_ref[h, s, :, cols]
            a = a_ref[pl.ds(r0, nk), cols]
            act = 0.5 * a * (1.0 + lax.erf(a * (2.0 ** -0.5)))
            w_ref[pl.ds(r0, nk), cols] = (gsum * act).astype(w_ref.dtype)
        return carry

    lax.fori_loop(0, eb // nk, gate_rows, 0)
    acc_ref[...] += jnp.dot(vt_ref[...], w_ref[...], preferred_element_type=F32)

    @pl.when(e == pl.num_programs(1) - 1)
    def _():
        o_ref[...] = acc_ref[...].T


def peer_experts(h2, u, v_t, r2, e2, n1, e1, tm=512, eb=1024):
    n, d = h2.shape
    n_exp = u.shape[0]
    n_heads, nk = r2.shape[0], r2.shape[1]
    tm, eb = min(tm, n), min(eb, n_exp)
    assert tm % LANES == 0 and eb % nk == 0
    tok_tile = pl.BlockSpec((n_heads, nk, tm), lambda i, e: (0, 0, i))
    n1 = n1.reshape(n_heads, nk, 1, n)
    e1 = e1.reshape(n_heads, nk, 1, n)
    i1_tile = pl.BlockSpec((n_heads, eb // nk, 1, tm), lambda i, e: (0, e, 0, i))
    return pl.pallas_call(
        functools.partial(_peer_expert_kernel, n_heads=n_heads, nk=nk),
        grid=(n // tm, n_exp // eb),
        in_specs=[pl.BlockSpec((tm, d), lambda i, e: (i, 0)),
                  pl.BlockSpec((eb, d), lambda i, e: (e, 0)),
                  pl.BlockSpec((d, eb), lambda i, e: (0, e)),
                  tok_tile, tok_tile, i1_tile, i1_tile],
        out_specs=pl.BlockSpec((tm, d), lambda i, e: (i, 0)),
        out_shape=jax.ShapeDtypeStruct((n, d), F32),
        scratch_shapes=[pltpu.VMEM((eb, tm), F32), pltpu.VMEM((eb, tm), BF16), pltpu.VMEM((d, tm), F32)],
        compiler_params=_cparams("arbitrary", "arbitrary"),
        name="peer_experts",
    )(h2, u, v_t, r2, e2, n1, e1)


def _residual_kernel(x_ref, gate_ref, y_ref, o_ref):
    o_ref[...] = x_ref[...] + gate_ref[...] * y_ref[...]


def gated_residual(x, gate, y, rows_per_group, tm=512):
    n, d = x.shape
    tm = min(tm, n)
    blk = pl.BlockSpec((tm, d), lambda i: (i, 0))
    return pl.pallas_call(
        _residual_kernel,
        grid=(n // tm,),
        in_specs=[blk, _mod_spec(gate, tm, rows_per_group), blk],
        out_specs=blk,
        out_shape=jax.ShapeDtypeStruct((n, d), F32),
        compiler_params=_cparams("arbitrary"),
        name="gated_residual",
    )(x, gate, y)


def _rope_tables(pos, half):
    freqs = ROPE_THETA ** (-jnp.arange(half, dtype=F32) / half)
    ang = pos.astype(F32)[:, None] * freqs[None, :]
    cos, sin = jnp.cos(ang), jnp.sin(ang)
    return jnp.concatenate([cos, cos], axis=1), jnp.concatenate([-sin, sin], axis=1)


def _channel_mixer(x_mid, shift2, scale2, gate2, rows_per_group, norm2_g, w_peer_q, keys_flat, peer_u, peer_vt,
                   n_peer_heads, tm):
    h2 = norm_modulate(x_mid, norm2_g, scale2, shift2, rows_per_group, tm)
    (pq_hm,) = project(h2, w_peer_q, 0, w_peer_q.shape[1], outs=(("head_major", BF16),))
    r2, e2, n1, e1 = peer_route(pq_hm, keys_flat, n_peer_heads)
    mixed = peer_experts(h2, peer_u, peer_vt, r2, e2, n1, e1, tm=tm)
    return gated_residual(x_mid, gate2, mixed, rows_per_group, tm=tm)


def kernel(x_prompt, x_sample, cache_k, cache_v, cache_idx_k, state_pool, page_table, c_prompt, c_sample,
           norm1_g, norm2_g, w_mod, b_mod, w_in, q_norm_g, k_norm_g, w_attn_proj, w_pool, pool_scale,
           w_out, w_peer_q, peer_keys, peer_u, peer_v):
    b, s, d = x_prompt.shape
    bd, t_new, _ = x_sample.shape
    hd = q_norm_g.shape[0]
    idx_dim = cache_idx_k.shape[2]
    page = cache_k.shape[1]
    n_kv = cache_k.shape[2]
    n_heads = w_attn_proj.shape[0] // hd
    d_pool = state_pool.shape[2]
    n_state = state_pool.shape[1]
    q_cols, kv_cols = n_heads * hd, n_kv * hd
    idx_heads = (w_in.shape[1] - q_cols - 2 * kv_cols - idx_dim - d_pool - 2 * d) // (idx_dim + 1)
    iq_cols = idx_heads * idx_dim
    past = page_table.shape[1] * page
    n_peer_heads, _, nk, half = peer_keys.shape
    assert hd == LANES and idx_dim == LANES and nk == LANES and half == LANES

    o_ik = q_cols + 2 * kv_cols + iq_cols
    o_iw = o_ik + idx_dim
    iw_pad = jnp.zeros((d, LANES - idx_heads), w_in.dtype)
    w_in_b = jnp.concatenate([w_in[:, :o_ik], w_in[:, o_iw + idx_heads:], w_in[:, o_ik:o_iw],
                              w_in[:, o_iw:o_iw + idx_heads], iw_pad], axis=1).astype(BF16)
    c_q, c_k, c_v = 0, q_cols, q_cols + kv_cols
    c_iq = c_v + kv_cols
    c_pool = c_iq + iq_cols
    c_ga = c_pool + d_pool
    c_gb = c_ga + d
    c_ik = c_gb + d
    c_iw = c_ik + idx_dim
    w_ap_b = w_attn_proj.astype(BF16)
    w_out_b = w_out.astype(BF16)
    w_pq_b = w_peer_q.astype(BF16)
    w_pool_b = w_pool.astype(BF16)
    keys_flat = peer_keys.reshape(n_peer_heads * 2, nk, half).astype(BF16)
    peer_u_b = peer_u.astype(BF16)
    peer_vt_b = peer_v.astype(BF16).T
    iw_scale = idx_heads ** -0.5 * idx_dim ** -0.5

    mod = modulation(jnp.concatenate([c_prompt, c_sample], axis=0), w_mod, b_mod)
    mod = mod.reshape(b + bd, 6, d)
    mod_p = [mod[:b, i][:, None, :] for i in range(6)]
    mod_s = [jnp.repeat(mod[b:, i], t_new, axis=0) for i in range(6)]

    def mixer_inputs(x2, shift, scale, pos, rows_per_group, tm):
        cos, sin = (jnp.tile(tbl, (x2.shape[0] // pos.shape[0], 1)) for tbl in _rope_tables(pos, hd // 2))
        h = norm_modulate(x2, norm1_g, scale, shift, rows_per_group, tm)
        (q_hm,) = project(h, w_in_b, c_q, q_cols, outs=(("head_major", BF16),), norm_g=q_norm_g, cos=cos, sin=sin, tm=tm)
        k_f, k_b = project(h, w_in_b, c_k, kv_cols, outs=(("token_major", F32), ("token_major", BF16)),
                           norm_g=k_norm_g, cos=cos, sin=sin, tm=tm)
        v_f, v_b = project(h, w_in_b, c_v, kv_cols, outs=(("token_major", F32), ("token_major", BF16)), tm=tm)
        (iq_hm,) = project(h, w_in_b, c_iq, iq_cols, outs=(("head_major", BF16),), cos=cos, sin=sin, tm=tm)
        ik_f, ik_b = project(h, w_in_b, c_ik, idx_dim, outs=(("token_major", F32), ("token_major", BF16)),
                             cos=cos, sin=sin, tm=tm)
        (iw,) = project(h, w_in_b, c_iw, LANES, outs=(("token_major", F32),), scale=iw_scale, tm=tm)
        (p_in,) = project(h, w_in_b, c_pool, d_pool, outs=(("token_major", F32),), tm=tm)
        (sig_a,) = project(h, w_in_b, c_ga, d, outs=(("token_major", BF16),), act="sigmoid", tm=tm)
        (sig_b,) = project(h, w_in_b, c_gb, d, outs=(("token_major", BF16),), act="sigmoid", tm=tm)
        return q_hm, k_f, k_b, v_f, v_b, iq_hm, ik_f, ik_b, iw[:, :idx_heads], p_in, sig_a, sig_b

    xp = x_prompt.reshape(b * s, d)
    tm_p = min(1024, s)
    pos_p = jnp.arange(s)
    (q_hm, k_f, k_b, v_f, v_b, iq_hm, ik_f, ik_b, iw, p_in, sig_a, sig_b) = mixer_inputs(
        xp, mod_p[0], mod_p[1], pos_p, s, tm_p)
    attn_p = prompt_attention(iq_hm, iw.T, ik_b, q_hm, k_b, v_b, b, s, min(TOPK_MAX, s // 4))
    pool_p = pool_prompt(p_in.reshape(b, s, d_pool), w_pool_b, pool_scale)
    merged = merge_branches(attn_p, w_ap_b, sig_a, sig_b, pool_p)
    h_p = out_project(merged, w_out_b, xp, mod_p[2], s)
    y_prompt = _channel_mixer(h_p, mod_p[3], mod_p[4], mod_p[5], s, norm2_g, w_pq_b, keys_flat, peer_u_b,
                              peer_vt_b, n_peer_heads, min(512, s))
    k_prompt = k_f.reshape(b, s, n_kv, hd)
    v_prompt = v_f.reshape(b, s, n_kv, hd)
    idx_k_prompt = ik_f.reshape(b, s, idx_dim)
    pool_state_prompt = p_in.reshape(b, s, d_pool)[:, s - n_state:, :]

    ns = bd * t_new
    xs = x_sample.reshape(ns, d)
    pos_s = past + jnp.arange(t_new)
    (q_hm, k_f, k_b, v_f, v_b, iq_hm, ik_f, ik_b, iw, p_in, sig_a, sig_b) = mixer_inputs(
        xs, mod_s[0], mod_s[1], pos_s, t_new, ns)
    group = n_heads // n_kv
    iq_rows = iq_hm.reshape(idx_heads, bd, t_new, idx_dim).transpose(1, 0, 2, 3).reshape(bd, idx_heads * t_new, idx_dim)
    iq_rows = jnp.pad(iq_rows, ((0, 0), (0, LANES - idx_heads * t_new), (0, 0)))
    iw_row = iw.reshape(bd, t_new, idx_heads).transpose(0, 2, 1).reshape(bd, 1, idx_heads * t_new)
    iw_row = jnp.pad(iw_row, ((0, 0), (0, 0), (0, LANES - idx_heads * t_new)))
    q5 = q_hm.reshape(n_kv, group, bd, t_new, hd).transpose(2, 0, 1, 3, 4)
    eye = jnp.eye(n_kv, dtype=q5.dtype)
    q_blk = (q5[:, :, :, :, None, :] * eye[None, :, None, None, :, None]).reshape(
        bd, n_kv * group * t_new, n_kv * hd)
    q_blk = jnp.pad(q_blk, ((0, 0), (0, LANES - n_kv * group * t_new), (0, 0)))
    new_rows = 8
    pad_new = lambda a: jnp.pad(a.reshape(bd, t_new, -1), ((0, 0), (0, new_rows - t_new), (0, 0)))
    att = sample_attention(iq_rows, iw_row, q_blk, pad_new(ik_f), pad_new(k_f), pad_new(v_f), cache_idx_k,
                           cache_k.reshape(-1, page, kv_cols), cache_v.reshape(-1, page, kv_cols),
                           page_table, t_new, min(TOPK_MAX, (past + t_new) // 4))
    att = att[:, :n_kv * group * t_new].reshape(bd, n_kv, group, t_new, n_kv, hd)
    att = jnp.einsum('bkgqkd->bqkgd', att).reshape(ns, q_cols).astype(BF16)
    pad_t = jnp.concatenate([state_pool, p_in.reshape(bd, t_new, d_pool)], axis=1)
    pool_s = pool_sample(pad_t.transpose(1, 0, 2), w_pool_b, pool_scale, t_new, past)
    pool_s = pool_s.transpose(1, 0, 2).reshape(ns, d)
    merged = merge_branches(att, w_ap_b, sig_a, sig_b, pool_s)
    h_s = out_project(merged, w_out_b, xs, mod_s[2], t_new)
    y_sample = _channel_mixer(h_s, mod_s[3], mod_s[4], mod_s[5], t_new, norm2_g, w_pq_b, keys_flat, peer_u_b,
                              peer_vt_b, n_peer_heads, ns)

    return (y_prompt.reshape(b, s, d), y_sample.reshape(bd, t_new, d), k_prompt, v_prompt, idx_k_prompt,
            pool_state_prompt, k_f.reshape(bd, t_new, n_kv, hd), v_f.reshape(bd, t_new, n_kv, hd),
            ik_f.reshape(bd, t_new, idx_dim), pad_t[:, t_new:, :])
```

```python
import functools
import math

import jax
import jax.numpy as jnp
from jax import lax
from jax.experimental import pallas as pl
from jax.experimental.pallas import tpu as pltpu

F32 = jnp.float32
BF16 = jnp.bfloat16
I32 = jnp.int32

LANES = 128
EPS = 1e-6
ROPE_THETA = 10000.0
TOPK_MAX = 256
POOL_WINDOWS = (2, 4, 8, 16)
POOL_HALO = 16
PEER_TOPK = 16
ROW_BLOCK = 16
VMEM_LIMIT = 56 * 1024 * 1024

NT_DIMS = (((1,), (1,)), ((), ()))
TN_DIMS = (((0,), (0,)), ((), ()))
NEG_BIG = -1e30
INT_MIN = -2 ** 31


def _cparams(*sem):
    return pltpu.CompilerParams(dimension_semantics=sem, vmem_limit_bytes=VMEM_LIMIT)


def _mod_kernel(c_ref, w_ref, b_ref, o_ref):
    c = c_ref[...]
    a = c * jax.nn.sigmoid(c)
    o_ref[...] = jnp.dot(a, w_ref[...], preferred_element_type=F32,
                         precision=lax.Precision.HIGHEST) + b_ref[...]


def modulation(c, w_mod, b_mod, tn=1024):
    n, d = c.shape
    cols = w_mod.shape[1]
    return pl.pallas_call(
        _mod_kernel,
        grid=(cols // tn,),
        in_specs=[pl.BlockSpec((n, d), lambda j: (0, 0)),
                  pl.BlockSpec((d, tn), lambda j: (0, j)),
                  pl.BlockSpec((1, tn), lambda j: (0, j))],
        out_specs=pl.BlockSpec((n, tn), lambda j: (0, j)),
        out_shape=jax.ShapeDtypeStruct((n, cols), F32),
        compiler_params=_cparams("arbitrary"),
        name="modulation",
    )(c, w_mod, b_mod.reshape(1, cols))


def _norm_mod_kernel(x_ref, g_ref, sc_ref, sh_ref, o_ref, *maybe_ot_ref):
    x = x_ref[...]
    y = x * lax.rsqrt(jnp.mean(x * x, axis=-1, keepdims=True) + EPS) * g_ref[...]
    h = y * (1.0 + sc_ref[...]) + sh_ref[...]
    o_ref[...] = h.astype(o_ref.dtype)
    for ot_ref in maybe_ot_ref:
        ot_ref[...] = h.T.astype(ot_ref.dtype)


def _mod_spec(arr, tm, rows_per_group):
    if arr.ndim == 3:
        bpg = rows_per_group // tm
        return pl.BlockSpec((None, 1, arr.shape[-1]), lambda i, *_: (i // bpg, 0, 0))
    return pl.BlockSpec((tm, arr.shape[-1]), lambda i, *_: (i, 0))


def norm_modulate(x, g, scale, shift, rows_per_group, tm, with_transpose=False):
    n, d = x.shape
    out_specs = [pl.BlockSpec((tm, d), lambda i: (i, 0))]
    out_shape = [jax.ShapeDtypeStruct((n, d), BF16)]
    if with_transpose:
        out_specs.append(pl.BlockSpec((d, tm), lambda i: (0, i)))
        out_shape.append(jax.ShapeDtypeStruct((d, n), BF16))
    return pl.pallas_call(
        _norm_mod_kernel,
        grid=(n // tm,),
        in_specs=[pl.BlockSpec((tm, d), lambda i: (i, 0)),
                  pl.BlockSpec((1, d), lambda i: (0, 0)),
                  _mod_spec(scale, tm, rows_per_group),
                  _mod_spec(shift, tm, rows_per_group)],
        out_specs=out_specs,
        out_shape=out_shape,
        compiler_params=_cparams("arbitrary"),
        name="norm_modulate",
    )(x, g.reshape(1, d), scale, shift)


def _proj_kernel(*refs, norm, rope, act, scale, outs):
    h_ref, w_ref = refs[0], refs[1]
    pos = 2
    g_ref = cos_ref = sin_ref = None
    if norm:
        g_ref = refs[pos]
        pos += 1
    if rope:
        cos_ref, sin_ref = refs[pos], refs[pos + 1]
        pos += 2
    out_refs = refs[pos:]
    acc = jnp.dot(h_ref[...], w_ref[...], preferred_element_type=F32)
    tn = acc.shape[1]
    if norm or rope:
        slabs = []
        for u in range(tn // LANES):
            t = acc[:, u * LANES:(u + 1) * LANES]
            if norm:
                t = t * lax.rsqrt(jnp.mean(t * t, axis=-1, keepdims=True) + EPS) * g_ref[...]
            if rope:
                t = t * cos_ref[...] + pltpu.roll(t, LANES // 2, 1) * sin_ref[...]
            slabs.append(t)
    else:
        if scale != 1.0:
            acc = acc * scale
        if act == "sigmoid":
            acc = jax.nn.sigmoid(acc)
        slabs = None
    for kind, o_ref in zip(outs, out_refs):
        if kind == "head_major":
            for u in range(tn // LANES):
                t = acc[:, u * LANES:(u + 1) * LANES] if slabs is None else slabs[u]
                o_ref[u] = t.astype(o_ref.dtype)
        elif slabs is not None:
            for u, t in enumerate(slabs):
                o_ref[:, u * LANES:(u + 1) * LANES] = t.astype(o_ref.dtype)
        else:
            o_ref[...] = acc.astype(o_ref.dtype)


def project(h, w, col0, ncols, *, outs, norm_g=None, cos=None, sin=None, act="none", scale=1.0,
            tm=1024, tn=512):
    n, k = h.shape
    tn = min(tn, ncols)
    tm = min(tm, n)
    assert col0 % tn == 0 and ncols % tn == 0 and n % tm == 0
    c0 = col0 // tn
    in_specs = [pl.BlockSpec((tm, k), lambda i, j: (i, 0)),
                pl.BlockSpec((k, tn), lambda i, j: (0, j + c0))]
    args = [h, w]
    if norm_g is not None:
        in_specs.append(pl.BlockSpec((1, LANES), lambda i, j: (0, 0)))
        args.append(norm_g.reshape(1, LANES))
    if cos is not None:
        in_specs += [pl.BlockSpec((tm, LANES), lambda i, j: (i, 0))] * 2
        args += [cos, sin]
    out_specs, out_shapes = [], []
    for kind, dt in outs:
        if kind == "head_major":
            out_specs.append(pl.BlockSpec((tn // LANES, tm, LANES), lambda i, j: (j, i, 0)))
            out_shapes.append(jax.ShapeDtypeStruct((ncols // LANES, n, LANES), dt))
        else:
            out_specs.append(pl.BlockSpec((tm, tn), lambda i, j: (i, j)))
            out_shapes.append(jax.ShapeDtypeStruct((n, ncols), dt))
    res = pl.pallas_call(
        functools.partial(_proj_kernel, norm=norm_g is not None, rope=cos is not None, act=act,
                          scale=scale, outs=tuple(kd for kd, _ in outs)),
        grid=(n // tm, ncols // tn),
        in_specs=in_specs,
        out_specs=out_specs,
        out_shape=out_shapes,
        compiler_params=_cparams("arbitrary", "arbitrary"),
        name="project",
    )(*args)
    return res


def _sortable_key(score):
    bits = pltpu.bitcast(score, I32)
    return jnp.where(bits < 0, bits ^ jnp.int32(0x7FFFFFFF), bits)


NEG_INF_KEY = (-8388608) ^ 0x7FFFFFFF


def _lane_group_sum(x, period):
    shift = period
    while shift < LANES:
        x = x + pltpu.roll(x, shift, 1)
        shift *= 2
    return x


def _topk_threshold(keys_ref, nchunks, ck, k, width, max_rows, period=None):
    acc_rows = math.gcd(ck, 64)

    def count(pred):
        def body(c, cnt):
            r0 = pl.multiple_of(c * ck, 8)
            kk = keys_ref[pl.ds(r0, ck), :]
            rows = r0 + lax.broadcasted_iota(I32, kk.shape, 0)
            hit = jnp.where(pred(kk, rows), 1.0, 0.0)
            return cnt + jnp.sum(hit.reshape(ck // acc_rows, acc_rows, width), axis=0)
        part = lax.fori_loop(0, nchunks, body, jnp.zeros((acc_rows, width), F32))
        return jnp.sum(part, axis=0, keepdims=True)

    if period is None:
        def bit_step(i, ut):
            uc = ut | jnp.left_shift(jnp.int32(1), jnp.int32(31) - i)
            cand = uc ^ jnp.int32(INT_MIN)
            return jnp.where(count(lambda kk, _: kk >= cand) >= float(k), uc, ut)
        ut = lax.fori_loop(0, 32, bit_step, jnp.zeros((1, width), I32))
    else:
        replica = lax.broadcasted_iota(I32, (1, width), 1) // period
        bits_per_pass = (width // period).bit_length() - 1
        ut = jnp.zeros((1, width), I32)
        hi = 32
        while hi > 0:
            nb = min(bits_per_pass, hi)
            lo = hi - nb
            cand = (ut | jnp.left_shift(replica, lo)) ^ jnp.int32(INT_MIN)
            ok = jnp.where(count(lambda kk, _, cand=cand: kk >= cand) >= float(k), 1.0, 0.0)
            ok = jnp.where(replica < 2 ** nb, ok, 0.0)
            best = _lane_group_sum(ok, period).astype(I32) - 1
            ut = ut | jnp.left_shift(best, lo)
            hi = lo
    thr = ut ^ jnp.int32(INT_MIN)

    real = thr > jnp.int32(NEG_INF_KEY)
    surplus = jnp.where(real, count(lambda kk, _: kk >= thr) - float(k), 0.0)

    @pl.when(jnp.max(surplus) > 0.0)
    def _():
        need = float(k) - count(lambda kk, _: kk > thr)
        n_bits = max(1, (max_rows - 1).bit_length())

        def idx_step(i, jv):
            cand = jv | jnp.left_shift(jnp.int32(1), jnp.int32(n_bits - 1) - i)
            below = count(lambda kk, rows: jnp.where(kk == thr, rows, jnp.int32(2 ** 30)) < cand)
            return jnp.where(below < need, cand, jv)
        last = lax.fori_loop(0, n_bits, idx_step, jnp.zeros((1, width), I32))
        last = jnp.where(surplus > 0.0, last, jnp.int32(2 ** 30))

        def demote(c, carry):
            r0 = pl.multiple_of(c * ck, 8)
            kk = keys_ref[pl.ds(r0, ck), :]
            rows = r0 + lax.broadcasted_iota(I32, kk.shape, 0)
            keys_ref[pl.ds(r0, ck), :] = jnp.where(jnp.where(kk == thr, rows, 0) > last, thr - 1, kk)
            return carry
        lax.fori_loop(0, nchunks, demote, 0)

    return jnp.maximum(thr, jnp.int32(NEG_INF_KEY + 1))


def _attn_prompt_kernel(iq_ref, iw_ref, ik_ref, q_ref, k_ref, v_ref, o_ref, keys_ref, *,
                        nq, ck, k_sel, n_idx_heads, n_kv, group, sm_scale):
    j = pl.program_id(1)
    nchunks = (j * nq) // ck + 1
    hd = LANES

    qpos = j * nq + lax.broadcasted_iota(I32, (ck, nq), 1)

    def score_chunk(c, carry):
        r0 = pl.multiple_of(c * ck, ck)
        ikc = ik_ref[pl.ds(r0, ck), :]
        acc = jnp.zeros((ck, nq), F32)
        for hp in range(n_idx_heads // 2):
            rhs = iq_ref[2 * hp:2 * hp + 2].reshape(2 * nq, hd)
            d = lax.dot_general(ikc, rhs, NT_DIMS, preferred_element_type=F32)
            for u in range(2):
                h = 2 * hp + u
                acc = acc + jnp.maximum(d[:, u * nq:(u + 1) * nq], 0.0) * iw_ref[h:h + 1, :]
        kpos = r0 + lax.broadcasted_iota(I32, (ck, nq), 0)
        score = jnp.where(kpos <= qpos, acc, -jnp.inf)
        keys_ref[pl.ds(r0, ck), :] = _sortable_key(score)
        return carry

    lax.fori_loop(0, nchunks, score_chunk, 0)

    thr = _topk_threshold(keys_ref, nchunks, ck, k_sel, nq, keys_ref.shape[0])
    thr_g = jnp.concatenate([thr] * group, axis=1)

    gw = group * nq
    exp2_scale = sm_scale * math.log2(math.e)
    for kvh in range(n_kv):
        qg = q_ref[group * kvh:group * (kvh + 1)].reshape(gw, hd)

        def attend(c, carry):
            m, l, acc = carry
            r0 = pl.multiple_of(c * ck, ck)
            kc = k_ref[pl.ds(r0, ck), kvh * hd:(kvh + 1) * hd]
            s = lax.dot_general(kc, qg, NT_DIMS, preferred_element_type=F32)
            kk = keys_ref[pl.ds(r0, ck), :]
            s = jnp.where(jnp.concatenate([kk] * group, axis=1) >= thr_g, s, NEG_BIG)
            m_new = jnp.maximum(m, jnp.max(s, axis=0, keepdims=True))
            alpha = jnp.exp2((m - m_new) * exp2_scale)
            p = jnp.exp2((s - m_new) * exp2_scale)
            l = alpha * l + jnp.sum(p, axis=0, keepdims=True)
            vc = v_ref[pl.ds(r0, ck), kvh * hd:(kvh + 1) * hd]
            pv = lax.dot_general(vc, p.astype(BF16), TN_DIMS, preferred_element_type=F32)
            return m_new, l, alpha * acc + pv

        init = (jnp.full((1, gw), NEG_BIG, F32), jnp.zeros((1, gw), F32), jnp.zeros((hd, gw), F32))
        _, l, acc = lax.fori_loop(0, nchunks, attend, init)
        out_t = acc / l
        for g in range(group):
            head = group * kvh + g
            o_ref[:, head * hd:(head + 1) * hd] = out_t[:, g * nq:(g + 1) * nq].T.astype(o_ref.dtype)


def prompt_attention(iq_hm, iw_t, ik, q_hm, k, v, batch, seq, k_sel, nq=128, ck=512):
    n_idx_heads = iq_hm.shape[0]
    n_heads = q_hm.shape[0]
    hd = LANES
    n_kv = k.shape[1] // hd
    group = n_heads // n_kv
    ck = min(ck, seq)
    nqb = seq // nq
    kern = functools.partial(_attn_prompt_kernel, nq=nq, ck=ck, k_sel=k_sel, n_idx_heads=n_idx_heads,
                             n_kv=n_kv, group=group, sm_scale=hd ** -0.5)
    return pl.pallas_call(
        kern,
        grid=(batch, nqb),
        in_specs=[pl.BlockSpec((n_idx_heads, nq, hd), lambda b, j: (0, b * nqb + j, 0)),
                  pl.BlockSpec((n_idx_heads, nq), lambda b, j: (0, b * nqb + j)),
                  pl.BlockSpec((seq, hd), lambda b, j: (b, 0)),
                  pl.BlockSpec((n_heads, nq, hd), lambda b, j: (0, b * nqb + j, 0)),
                  pl.BlockSpec((seq, n_kv * hd), lambda b, j: (b, 0)),
                  pl.BlockSpec((seq, n_kv * hd), lambda b, j: (b, 0))],
        out_specs=pl.BlockSpec((nq, n_heads * hd), lambda b, j: (b * nqb + j, 0)),
        out_shape=jax.ShapeDtypeStruct((batch * seq, n_heads * hd), BF16),
        scratch_shapes=[pltpu.VMEM((seq, nq), I32)],
        compiler_params=_cparams("arbitrary", "arbitrary"),
        name="prompt_attention",
    )(iq_hm, iw_t, ik, q_hm, k, v)


def _attn_sample_kernel(pt_ref, iq_ref, iw_ref, q_ref, ikn_ref, kn_ref, vn_ref, *rest,
                        n_pages, page, t_new, k_sel, sm_scale):
    del pt_ref
    ik_pages = rest[:n_pages]
    k_pages = rest[n_pages:2 * n_pages]
    v_pages = rest[2 * n_pages:3 * n_pages]
    o_ref = rest[3 * n_pages]
    keys_ref = rest[3 * n_pages + 1]
    s_ref = rest[3 * n_pages + 2]
    past = n_pages * page
    new_rows = ikn_ref.shape[0]
    nrows = past + new_rows

    iq = iq_ref[...]
    iw = iw_ref[...]

    lane_in = lax.broadcasted_iota(I32, (LANES, LANES), 0) % t_new
    lane_out = lax.broadcasted_iota(I32, (LANES, LANES), 1) % t_new
    head_sum = jnp.where(lane_in == lane_out, 1.0, 0.0)

    def scores(ik_rows):
        d = lax.dot_general(ik_rows.astype(BF16), iq, NT_DIMS, preferred_element_type=F32)
        return jnp.dot(jnp.maximum(d, 0.0) * iw, head_sum, preferred_element_type=F32,
                       precision=lax.Precision.HIGHEST)

    for p in range(n_pages):
        keys_ref[p * page:(p + 1) * page, :] = _sortable_key(scores(ik_pages[p][...]))
    sc_new = scores(ikn_ref[...])
    row = lax.broadcasted_iota(I32, (new_rows, LANES), 0)
    lane_q = lax.broadcasted_iota(I32, (new_rows, LANES), 1) % t_new
    sc_new = jnp.where(row <= lane_q, sc_new, -jnp.inf)
    keys_ref[past:nrows, :] = _sortable_key(sc_new)
    keys_ref[nrows:, :] = jnp.full((keys_ref.shape[0] - nrows, LANES), NEG_INF_KEY, I32)

    thr = _topk_threshold(keys_ref, 1, keys_ref.shape[0], k_sel, LANES, keys_ref.shape[0], period=t_new)

    qb = q_ref[...]
    kvw = qb.shape[1]
    m = jnp.full((1, LANES), NEG_BIG, F32)
    blocks = [(k_pages[p], v_pages[p], p * page, page) for p in range(n_pages)]
    blocks.append((kn_ref, vn_ref, past, new_rows))
    for k_blk, _, r0, rows in blocks:
        s = lax.dot_general(k_blk[...].astype(BF16), qb, NT_DIMS, preferred_element_type=F32) * sm_scale
        s = jnp.where(keys_ref[r0:r0 + rows, :] >= thr, s, NEG_BIG)
        s_ref[r0:r0 + rows, :] = s
        m = jnp.maximum(m, jnp.max(s, axis=0, keepdims=True))
    acc = jnp.zeros((LANES, kvw), F32)
    lsum = jnp.zeros((LANES, LANES), F32)
    for _, v_blk, r0, rows in blocks:
        s = s_ref[r0:r0 + rows, :]
        pr = jnp.where(s > 0.5 * NEG_BIG, jnp.exp(s - m), 0.0).astype(BF16)
        acc = acc + lax.dot_general(pr, v_blk[...].astype(BF16), TN_DIMS, preferred_element_type=F32)
        lsum = lsum + lax.dot_general(pr, jnp.ones((rows, LANES), BF16), TN_DIMS, preferred_element_type=F32)
    o_ref[...] = acc / jnp.concatenate([lsum] * (kvw // LANES), axis=1)


def sample_attention(iq_rows, iw_row, q_blk, ik_new, k_new, v_new, cache_idx_k, cache_k2, cache_v2,
                     page_table, t_new, k_sel):
    bd, n_pages = page_table.shape
    page = cache_idx_k.shape[1]
    idx_dim = cache_idx_k.shape[2]
    kvw = cache_k2.shape[2]
    qw = q_blk.shape[1]
    assert qw == LANES
    new_rows = ik_new.shape[1]
    nrows = n_pages * page + new_rows

    def page_spec(width, p):
        return pl.BlockSpec((None, page, width), lambda b, pt: (pt[b * n_pages + p], 0, 0))

    def per_seq(shape):
        return pl.BlockSpec((None,) + shape, lambda b, pt: (b, 0, 0))

    in_specs = [per_seq((LANES, idx_dim)), per_seq((1, LANES)), per_seq((qw, kvw)),
                per_seq((new_rows, idx_dim)), per_seq((new_rows, kvw)), per_seq((new_rows, kvw))]
    in_specs += [page_spec(idx_dim, p) for p in range(n_pages)]
    in_specs += [page_spec(kvw, p) for p in range(n_pages)]
    in_specs += [page_spec(kvw, p) for p in range(n_pages)]
    kern = functools.partial(_attn_sample_kernel, n_pages=n_pages, page=page, t_new=t_new, k_sel=k_sel,
                             sm_scale=LANES ** -0.5)
    return pl.pallas_call(
        kern,
        grid_spec=pltpu.PrefetchScalarGridSpec(
            num_scalar_prefetch=1,
            grid=(bd,),
            in_specs=in_specs,
            out_specs=pl.BlockSpec((None, qw, kvw), lambda b, pt: (b, 0, 0)),
            scratch_shapes=[pltpu.VMEM((-(-nrows // 64) * 64, LANES), I32), pltpu.VMEM((nrows, LANES), F32)]),
        out_shape=jax.ShapeDtypeStruct((bd, qw, kvw), F32),
        compiler_params=_cparams("arbitrary"),
        name="sample_attention",
    )(page_table.reshape(-1), iq_rows, iw_row, q_blk, ik_new, k_new, v_new,
      *([cache_idx_k] * n_pages), *([cache_k2] * n_pages), *([cache_v2] * n_pages))


def _pool_prompt_kernel(p_ref, halo_ref, w_ref, ps_ref, o_ref, ext_ref, *, tm):
    sblk = pl.program_id(1)
    c = p_ref.shape[-1]
    gc = c // len(POOL_WINDOWS)
    od = o_ref.shape[-1] // len(POOL_WINDOWS)
    ext_ref[0:POOL_HALO, :] = jnp.where(sblk > 0, halo_ref[...], 0.0)
    ext_ref[POOL_HALO:POOL_HALO + tm, :] = p_ref[...]
    pos = sblk * tm + lax.broadcasted_iota(I32, (tm, gc), 0)
    for g, w in enumerate(POOL_WINDOWS):
        cols = slice(g * gc, (g + 1) * gc)
        tot = ext_ref[POOL_HALO:POOL_HALO + tm, cols]
        for dlt in range(1, w):
            tot = tot + ext_ref[POOL_HALO - dlt:POOL_HALO - dlt + tm, cols]
        cnt = jnp.minimum(pos + 1, w).astype(F32)
        pooled = tot / cnt - p_ref[:, cols]
        out = jnp.dot(pooled.astype(BF16), w_ref[g], preferred_element_type=F32)
        o_ref[:, g * od:(g + 1) * od] = (out * ps_ref[:, g * od:(g + 1) * od]).astype(o_ref.dtype)


def pool_prompt(p_in, w_pool, pool_scale, tm=512):
    b, s, c = p_in.shape
    g, gc, od = w_pool.shape
    tm = min(tm, s)
    nsb = s // tm
    hb = tm // POOL_HALO
    return pl.pallas_call(
        functools.partial(_pool_prompt_kernel, tm=tm),
        grid=(b, nsb),
        in_specs=[pl.BlockSpec((None, tm, c), lambda bi, si: (bi, si, 0)),
                  pl.BlockSpec((None, POOL_HALO, c), lambda bi, si: (bi, jnp.maximum(si * hb - 1, 0), 0)),
                  pl.BlockSpec((g, gc, od), lambda bi, si: (0, 0, 0)),
                  pl.BlockSpec((1, g * od), lambda bi, si: (0, 0))],
        out_specs=pl.BlockSpec((tm, g * od), lambda bi, si: (bi * nsb + si, 0)),
        out_shape=jax.ShapeDtypeStruct((b * s, g * od), F32),
        scratch_shapes=[pltpu.VMEM((POOL_HALO + tm, c), F32)],
        compiler_params=_cparams("arbitrary", "arbitrary"),
        name="pool_prompt",
    )(p_in, p_in, w_pool, pool_scale.reshape(1, g * od))


def _pool_sample_kernel(pad_ref, w_ref, ps_ref, o_ref, *, t_new, past):
    n_state = pad_ref.shape[0] - t_new
    c = pad_ref.shape[-1]
    gc = c // len(POOL_WINDOWS)
    od = o_ref.shape[-1] // len(POOL_WINDOWS)
    for t in range(t_new):
        for g, w in enumerate(POOL_WINDOWS):
            cols = slice(g * gc, (g + 1) * gc)
            cur = pad_ref[n_state + t, :, cols]
            tot = cur
            for dlt in range(1, w):
                tot = tot + pad_ref[n_state + t - dlt, :, cols]
            pooled = tot / float(min(past + t + 1, w)) - cur
            out = jnp.dot(pooled.astype(BF16), w_ref[g], preferred_element_type=F32)
            o_ref[t, :, g * od:(g + 1) * od] = out * ps_ref[:, g * od:(g + 1) * od]


def pool_sample(pad_t, w_pool, pool_scale, t_new, past):
    _, bd, c = pad_t.shape
    g, gc, od = w_pool.shape
    return pl.pallas_call(
        functools.partial(_pool_sample_kernel, t_new=t_new, past=past),
        out_shape=jax.ShapeDtypeStruct((t_new, bd, g * od), F32),
        compiler_params=pltpu.CompilerParams(vmem_limit_bytes=VMEM_LIMIT),
        name="pool_sample",
    )(pad_t, w_pool, pool_scale.reshape(1, g * od))


def _merge_kernel(a_ref, w_ref, ga_ref, gb_ref, pool_ref, o_ref):
    acc = jnp.dot(a_ref[...], w_ref[...], preferred_element_type=F32)
    o_ref[...] = (ga_ref[...].astype(F32) * acc + gb_ref[...].astype(F32) * pool_ref[...]).astype(o_ref.dtype)


def merge_branches(attn, w_attn_proj, sig_a, sig_b, pool_out, tm=1024, tn=512):
    n, k = attn.shape
    d = w_attn_proj.shape[1]
    tm, tn = min(tm, n), min(tn, d)
    blk = pl.BlockSpec((tm, tn), lambda i, j: (i, j))
    return pl.pallas_call(
        _merge_kernel,
        grid=(n // tm, d // tn),
        in_specs=[pl.BlockSpec((tm, k), lambda i, j: (i, 0)),
                  pl.BlockSpec((k, tn), lambda i, j: (0, j)), blk, blk, blk],
        out_specs=blk,
        out_shape=jax.ShapeDtypeStruct((n, d), BF16),
        compiler_params=_cparams("arbitrary", "arbitrary"),
        name="merge_branches",
    )(attn, w_attn_proj, sig_a, sig_b, pool_out)


def _out_proj_kernel(m_ref, w_ref, x_ref, gate_ref, o_ref):
    acc = jnp.dot(m_ref[...], w_ref[...], preferred_element_type=F32)
    o_ref[...] = x_ref[...] + gate_ref[...] * acc


def out_project(merged, w_out, x, gate, rows_per_group, tm=1024, tn=512):
    n, k = merged.shape
    d = w_out.shape[1]
    tm, tn = min(tm, n), min(tn, d)
    if gate.ndim == 3:
        tm = min(tm, rows_per_group)
        bpg = rows_per_group // tm
        gate_spec = pl.BlockSpec((None, 1, tn), lambda i, j: (i // bpg, 0, j))
    else:
        gate_spec = pl.BlockSpec((tm, tn), lambda i, j: (i, j))
    blk = pl.BlockSpec((tm, tn), lambda i, j: (i, j))
    return pl.pallas_call(
        _out_proj_kernel,
        grid=(n // tm, d // tn),
        in_specs=[pl.BlockSpec((tm, k), lambda i, j: (i, 0)),
                  pl.BlockSpec((k, tn), lambda i, j: (0, j)), blk, gate_spec],
        out_specs=blk,
        out_shape=jax.ShapeDtypeStruct((n, d), F32),
        compiler_params=_cparams("arbitrary", "arbitrary"),
        name="out_project",
    )(merged, w_out, x, gate)


def _top_lists(s, n_top):
    rows, t = s.shape
    iota = lax.broadcasted_iota(I32, (rows, t), 0).astype(F32)
    rank = jnp.full((rows, t), float(rows), F32)
    work = s
    vals = []
    for r in range(n_top):
        m = jnp.max(work, axis=0, keepdims=True)
        idx = jnp.min(jnp.where(work == m, iota, float(rows)), axis=0, keepdims=True)
        hit = iota == idx
        work = jnp.where(hit, -jnp.inf, work)
        rank = jnp.where(hit, float(r), rank)
        vals.append(m)
    return jnp.concatenate(vals, axis=0), rank


def _peer_route_kernel(pq_ref, keys_ref, r2_ref, e2_ref, n1_ref, e1_ref, *, n_heads, n_top):
    t = pq_ref.shape[1]
    pieces = []
    for a in range(n_top):
        nb = n_top // (a + 1)
        rows = -(-nb // 8) * 8
        pieces.append((a, nb, rows))
    flat_idx = jnp.concatenate(
        [a * n_top + lax.broadcasted_iota(I32, (rows, t), 0) for a, _, rows in pieces], axis=0).astype(F32)
    n_cand = flat_idx.shape[0]

    def route_head(h, carry):
        s1 = lax.dot_general(keys_ref[2 * h], pq_ref[2 * h], NT_DIMS, preferred_element_type=F32)
        s2 = lax.dot_general(keys_ref[2 * h + 1], pq_ref[2 * h + 1], NT_DIMS, preferred_element_type=F32)
        v1, rank1 = _top_lists(s1, n_top)
        v2, rank2 = _top_lists(s2, n_top)
        cand = jnp.concatenate(
            [jnp.where(lax.broadcasted_iota(I32, (rows, t), 0) < nb, v1[a:a + 1] + v2[:rows], -jnp.inf)
             for a, nb, rows in pieces], axis=0)
        chosen = jnp.zeros((n_cand, t), F32)
        top = cand[0:1]
        z = jnp.zeros((1, t), F32)
        for _ in range(n_top):
            m = jnp.max(cand, axis=0, keepdims=True)
            idx = jnp.min(jnp.where(cand == m, flat_idx, float(n_top * n_top)), axis=0, keepdims=True)
            hit = flat_idx == idx
            cand = jnp.where(hit, -jnp.inf, cand)
            chosen = jnp.where(hit, 1.0, chosen)
            z = z + jnp.exp(m - top)
        n1 = jnp.zeros_like(rank1)
        r0 = 0
        for a, _, rows in pieces:
            n_a = jnp.sum(chosen[r0:r0 + rows], axis=0, keepdims=True)
            n1 = jnp.where(rank1 == float(a), n_a, n1)
            r0 += rows
        inv_z = 1.0 / z
        r2_ref[h] = rank2.astype(r2_ref.dtype)
        e2_ref[h] = jnp.where(rank2 < float(n_top), jnp.exp(s2 - v2[0:1]), 0.0).astype(e2_ref.dtype)
        n1_ref[h] = n1
        e1_ref[h] = jnp.where(rank1 < float(n_top), jnp.exp(s1 - v1[0:1]), 0.0) * inv_z
        return carry

    lax.fori_loop(0, n_heads, route_head, 0)


def peer_route(pq_hm, keys_flat, n_heads, tp=256):
    n = pq_hm.shape[1]
    nk, half = keys_flat.shape[1], keys_flat.shape[2]
    tp = min(tp, n)
    out_spec = pl.BlockSpec((n_heads, nk, tp), lambda i: (0, 0, i))
    return pl.pallas_call(
        functools.partial(_peer_route_kernel, n_heads=n_heads, n_top=PEER_TOPK),
        grid=(n // tp,),
        in_specs=[pl.BlockSpec((2 * n_heads, tp, half), lambda i: (0, i, 0)),
                  pl.BlockSpec(keys_flat.shape, lambda i: (0, 0, 0))],
        out_specs=[out_spec] * 4,
        out_shape=[jax.ShapeDtypeStruct((n_heads, nk, n), dt) for dt in (BF16, BF16, F32, F32)],
        compiler_params=_cparams("arbitrary"),
        name="peer_route",
    )(pq_hm, keys_flat)


def _peer_expert_kernel(ht_ref, u_ref, vt_ref, r2_ref, e2_ref, n1_ref, e1_ref, o_ref,
                        a_ref, w_ref, acc_ref, n1s_ref, e1s_ref, *, n_heads, nk):
    e = pl.program_id(1)
    eb, tm = a_ref.shape
    n_sub = eb // nk

    @pl.when(e == 0)
    def _():
        acc_ref[...] = jnp.zeros_like(acc_ref)

    a_ref[...] = jnp.dot(u_ref[...], ht_ref[...], preferred_element_type=F32)

    for h in range(n_heads):
        for s in range(n_sub):
            n1s_ref[h, s] = jnp.broadcast_to(n1_ref[h, s:s + 1, :], (ROW_BLOCK, tm)).astype(BF16)
            e1s_ref[h, s] = jnp.broadcast_to(e1_ref[h, s:s + 1, :], (ROW_BLOCK, tm)).astype(BF16)

    def sub_block(s, carry):
        r0 = pl.multiple_of(s * nk, nk)
        for rb in range(nk // ROW_BLOCK):
            rr = slice(rb * ROW_BLOCK, (rb + 1) * ROW_BLOCK)
            gsum = jnp.zeros((ROW_BLOCK, tm), BF16)
            for h in range(n_heads):
                keep = r2_ref[h, rr, :] < n1s_ref[h, s]
                gsum = gsum + jnp.where(keep, e2_ref[h, rr, :], jnp.zeros((), BF16)) * e1s_ref[h, s]
            a = a_ref[pl.ds(r0 + rb * ROW_BLOCK, ROW_BLOCK), :]
            act = 0.5 * a * (1.0 + lax.erf(a * (2.0 ** -0.5)))
            w_ref[pl.ds(r0 + rb * ROW_BLOCK, ROW_BLOCK), :] = gsum * act.astype(BF16)
        return carry

    lax.fori_loop(0, n_sub, sub_block, 0)
    acc_ref[...] += jnp.dot(vt_ref[...], w_ref[...], preferred_element_type=F32)

    @pl.when(e == pl.num_programs(1) - 1)
    def _():
        o_ref[...] = acc_ref[...].T


def peer_experts(h2_t, u, v_t, r2, e2, n1, e1, tm=512, eb=1024):
    d, n = h2_t.shape
    n_exp = u.shape[0]
    n_heads, nk = r2.shape[0], r2.shape[1]
    tm, eb = min(tm, n), min(eb, n_exp)
    assert tm % LANES == 0 and eb % nk == 0 and (eb // nk) % 8 == 0 and nk % ROW_BLOCK == 0
    n_sub = eb // nk
    tok_tile = pl.BlockSpec((n_heads, nk, tm), lambda i, e: (0, 0, i))
    i1_tile = pl.BlockSpec((n_heads, n_sub, tm), lambda i, e: (0, e, i))
    return pl.pallas_call(
        functools.partial(_peer_expert_kernel, n_heads=n_heads, nk=nk),
        grid=(n // tm, n_exp // eb),
        in_specs=[pl.BlockSpec((d, tm), lambda i, e: (0, i)),
                  pl.BlockSpec((eb, d), lambda i, e: (e, 0)),
                  pl.BlockSpec((d, eb), lambda i, e: (0, e)),
                  tok_tile, tok_tile, i1_tile, i1_tile],
        out_specs=pl.BlockSpec((tm, d), lambda i, e: (i, 0)),
        out_shape=jax.ShapeDtypeStruct((n, d), F32),
        scratch_shapes=[pltpu.VMEM((eb, tm), F32), pltpu.VMEM((eb, tm), BF16), pltpu.VMEM((d, tm), F32),
                        pltpu.VMEM((n_heads, n_sub, ROW_BLOCK, tm), BF16),
                        pltpu.VMEM((n_heads, n_sub, ROW_BLOCK, tm), BF16)],
        compiler_params=_cparams("arbitrary", "arbitrary"),
        name="peer_experts",
    )(h2_t, u, v_t, r2, e2, n1, e1)


def _residual_kernel(x_ref, gate_ref, y_ref, o_ref):
    o_ref[...] = x_ref[...] + gate_ref[...] * y_ref[...]


def gated_residual(x, gate, y, rows_per_group, tm=512):
    n, d = x.shape
    tm = min(tm, n)
    blk = pl.BlockSpec((tm, d), lambda i: (i, 0))
    return pl.pallas_call(
        _residual_kernel,
        grid=(n // tm,),
        in_specs=[blk, _mod_spec(gate, tm, rows_per_group), blk],
        out_specs=blk,
        out_shape=jax.ShapeDtypeStruct((n, d), F32),
        compiler_params=_cparams("arbitrary"),
        name="gated_residual",
    )(x, gate, y)


def _rope_tables(pos, half):
    freqs = ROPE_THETA ** (-jnp.arange(half, dtype=F32) / half)
    ang = pos.astype(F32)[:, None] * freqs[None, :]
    cos, sin = jnp.cos(ang), jnp.sin(ang)
    return jnp.concatenate([cos, cos], axis=1), jnp.concatenate([-sin, sin], axis=1)


def _channel_mixer(x_mid, shift2, scale2, gate2, rows_per_group, norm2_g, w_peer_q, keys_flat, peer_u, peer_vt,
                   n_peer_heads, tm):
    h2, h2_t = norm_modulate(x_mid, norm2_g, scale2, shift2, rows_per_group, tm, with_transpose=True)
    (pq_hm,) = project(h2, w_peer_q, 0, w_peer_q.shape[1], outs=(("head_major", BF16),))
    r2, e2, n1, e1 = peer_route(pq_hm, keys_flat, n_peer_heads)
    mixed = peer_experts(h2_t, peer_u, peer_vt, r2, e2, n1, e1, tm=tm)
    return gated_residual(x_mid, gate2, mixed, rows_per_group, tm=tm)


def kernel(x_prompt, x_sample, cache_k, cache_v, cache_idx_k, state_pool, page_table, c_prompt, c_sample,
           norm1_g, norm2_g, w_mod, b_mod, w_in, q_norm_g, k_norm_g, w_attn_proj, w_pool, pool_scale,
           w_out, w_peer_q, peer_keys, peer_u, peer_v):
    b, s, d = x_prompt.shape
    bd, t_new, _ = x_sample.shape
    hd = q_norm_g.shape[0]
    idx_dim = cache_idx_k.shape[2]
    page = cache_k.shape[1]
    n_kv = cache_k.shape[2]
    n_heads = w_attn_proj.shape[0] // hd
    d_pool = state_pool.shape[2]
    n_state = state_pool.shape[1]
    q_cols, kv_cols = n_heads * hd, n_kv * hd
    idx_heads = (w_in.shape[1] - q_cols - 2 * kv_cols - idx_dim - d_pool - 2 * d) // (idx_dim + 1)
    iq_cols = idx_heads * idx_dim
    past = page_table.shape[1] * page
    n_peer_heads, _, nk, half = peer_keys.shape
    assert hd == LANES and idx_dim == LANES and nk == LANES and half == LANES

    o_ik = q_cols + 2 * kv_cols + iq_cols
    o_iw = o_ik + idx_dim
    iw_pad = jnp.zeros((d, LANES - idx_heads), w_in.dtype)
    w_in_b = jnp.concatenate([w_in[:, :o_ik], w_in[:, o_iw + idx_heads:], w_in[:, o_ik:o_iw],
                              w_in[:, o_iw:o_iw + idx_heads], iw_pad], axis=1).astype(BF16)
    c_q, c_k, c_v = 0, q_cols, q_cols + kv_cols
    c_iq = c_v + kv_cols
    c_pool = c_iq + iq_cols
    c_ga = c_pool + d_pool
    c_gb = c_ga + d
    c_ik = c_gb + d
    c_iw = c_ik + idx_dim
    w_ap_b = w_attn_proj.astype(BF16)
    w_out_b = w_out.astype(BF16)
    w_pq_b = w_peer_q.astype(BF16)
    w_pool_b = w_pool.astype(BF16)
    keys_flat = peer_keys.reshape(n_peer_heads * 2, nk, half).astype(BF16)
    peer_u_b = peer_u.astype(BF16)
    peer_vt_b = peer_v.astype(BF16).T
    iw_scale = idx_heads ** -0.5 * idx_dim ** -0.5

    mod = modulation(jnp.concatenate([c_prompt, c_sample], axis=0), w_mod, b_mod)
    mod = mod.reshape(b + bd, 6, d)
    mod_p = [mod[:b, i][:, None, :] for i in range(6)]
    mod_s = [jnp.repeat(mod[b:, i], t_new, axis=0) for i in range(6)]

    def mixer_inputs(x2, shift, scale, pos, rows_per_group, tm):
        cos, sin = (jnp.tile(tbl, (x2.shape[0] // pos.shape[0], 1)) for tbl in _rope_tables(pos, hd // 2))
        (h,) = norm_modulate(x2, norm1_g, scale, shift, rows_per_group, tm)
        (q_hm,) = project(h, w_in_b, c_q, q_cols, outs=(("head_major", BF16),), norm_g=q_norm_g, cos=cos, sin=sin, tm=tm)
        k_f, k_b = project(h, w_in_b, c_k, kv_cols, outs=(("token_major", F32), ("token_major", BF16)),
                           norm_g=k_norm_g, cos=cos, sin=sin, tm=tm)
        v_f, v_b = project(h, w_in_b, c_v, kv_cols, outs=(("token_major", F32), ("token_major", BF16)), tm=tm)
        (iq_hm,) = project(h, w_in_b, c_iq, iq_cols, outs=(("head_major", BF16),), cos=cos, sin=sin, tm=tm)
        ik_f, ik_b = project(h, w_in_b, c_ik, idx_dim, outs=(("token_major", F32), ("token_major", BF16)),
                             cos=cos, sin=sin, tm=tm)
        (iw,) = project(h, w_in_b, c_iw, LANES, outs=(("token_major", F32),), scale=iw_scale, tm=tm)
        (p_in,) = project(h, w_in_b, c_pool, d_pool, outs=(("token_major", F32),), tm=tm)
        (sig_a,) = project(h, w_in_b, c_ga, d, outs=(("token_major", BF16),), act="sigmoid", tm=tm)
        (sig_b,) = project(h, w_in_b, c_gb, d, outs=(("token_major", BF16),), act="sigmoid", tm=tm)
        return q_hm, k_f, k_b, v_f, v_b, iq_hm, ik_f, ik_b, iw[:, :idx_heads], p_in, sig_a, sig_b

    xp = x_prompt.reshape(b * s, d)
    tm_p = min(1024, s)
    pos_p = jnp.arange(s)
    (q_hm, k_f, k_b, v_f, v_b, iq_hm, ik_f, ik_b, iw, p_in, sig_a, sig_b) = mixer_inputs(
        xp, mod_p[0], mod_p[1], pos_p, s, tm_p)
    attn_p = prompt_attention(iq_hm, iw.T, ik_b, q_hm, k_b, v_b, b, s, min(TOPK_MAX, s // 4))
    pool_p = pool_prompt(p_in.reshape(b, s, d_pool), w_pool_b, pool_scale)
    merged = merge_branches(attn_p, w_ap_b, sig_a, sig_b, pool_p)
    h_p = out_project(merged, w_out_b, xp, mod_p[2], s)
    y_prompt = _channel_mixer(h_p, mod_p[3], mod_p[4], mod_p[5], s, norm2_g, w_pq_b, keys_flat, peer_u_b,
                              peer_vt_b, n_peer_heads, min(512, s))
    k_prompt = k_f.reshape(b, s, n_kv, hd)
    v_prompt = v_f.reshape(b, s, n_kv, hd)
    idx_k_prompt = ik_f.reshape(b, s, idx_dim)
    pool_state_prompt = p_in.reshape(b, s, d_pool)[:, s - n_state:, :]

    ns = bd * t_new
    xs = x_sample.reshape(ns, d)
    pos_s = past + jnp.arange(t_new)
    (q_hm, k_f, k_b, v_f, v_b, iq_hm, ik_f, ik_b, iw, p_in, sig_a, sig_b) = mixer_inputs(
        xs, mod_s[0], mod_s[1], pos_s, t_new, ns)
    group = n_heads // n_kv
    iq_rows = iq_hm.reshape(idx_heads, bd, t_new, idx_dim).transpose(1, 0, 2, 3).reshape(bd, idx_heads * t_new, idx_dim)
    iq_rows = jnp.pad(iq_rows, ((0, 0), (0, LANES - idx_heads * t_new), (0, 0)))
    iw_row = iw.reshape(bd, t_new, idx_heads).transpose(0, 2, 1).reshape(bd, 1, idx_heads * t_new)
    iw_row = jnp.pad(iw_row, ((0, 0), (0, 0), (0, LANES - idx_heads * t_new)))
    q5 = q_hm.reshape(n_kv, group, bd, t_new, hd).transpose(2, 0, 1, 3, 4)
    eye = jnp.eye(n_kv, dtype=q5.dtype)
    q_blk = (q5[:, :, :, :, None, :] * eye[None, :, None, None, :, None]).reshape(
        bd, n_kv * group * t_new, n_kv * hd)
    q_blk = jnp.pad(q_blk, ((0, 0), (0, LANES - n_kv * group * t_new), (0, 0)))
    new_rows = 8
    pad_new = lambda a: jnp.pad(a.reshape(bd, t_new, -1), ((0, 0), (0, new_rows - t_new), (0, 0)))
    att = sample_attention(iq_rows, iw_row, q_blk, pad_new(ik_f), pad_new(k_f), pad_new(v_f), cache_idx_k,
                           cache_k.reshape(-1, page, kv_cols), cache_v.reshape(-1, page, kv_cols),
                           page_table, t_new, min(TOPK_MAX, (past + t_new) // 4))
    att = att[:, :n_kv * group * t_new].reshape(bd, n_kv, group, t_new, n_kv, hd)
    att = jnp.einsum('bkgqkd->bqkgd', att).reshape(ns, q_cols).astype(BF16)
    pad_t = jnp.concatenate([state_pool, p_in.reshape(bd, t_new, d_pool)], axis=1)
    pool_s = pool_sample(pad_t.transpose(1, 0, 2), w_pool_b, pool_scale, t_new, past)
    pool_s = pool_s.transpose(1, 0, 2).reshape(ns, d)
    merged = merge_branches(att, w_ap_b, sig_a, sig_b, pool_s)
    h_s = out_project(merged, w_out_b, xs, mod_s[2], t_new)
    y_sample = _channel_mixer(h_s, mod_s[3], mod_s[4], mod_s[5], t_new, norm2_g, w_pq_b, keys_flat, peer_u_b,
                              peer_vt_b, n_peer_heads, ns)

    return (y_prompt.reshape(b, s, d), y_sample.reshape(bd, t_new, d), k_prompt, v_prompt, idx_k_prompt,
            pool_state_prompt, k_f.reshape(bd, t_new, n_kv, hd), v_f.reshape(bd, t_new, n_kv, hd),
            ik_f.reshape(bd, t_new, idx_dim), pad_t[:, t_new:, :])
```

```python
import functools
import math

import jax
import jax.numpy as jnp
from jax import lax
from jax.experimental import pallas as pl
from jax.experimental.pallas import tpu as pltpu

F32 = jnp.float32
BF16 = jnp.bfloat16
I32 = jnp.int32

LANES = 128
EPS = 1e-6
ROPE_THETA = 10000.0
TOPK_MAX = 256
POOL_WINDOWS = (2, 4, 8, 16)
POOL_HALO = 16
PEER_TOPK = 16
ROW_BLOCK = 16
VMEM_LIMIT = 56 * 1024 * 1024

NT_DIMS = (((1,), (1,)), ((), ()))
TN_DIMS = (((0,), (0,)), ((), ()))
NEG_BIG = -1e30
INT_MIN = -2 ** 31


def _cparams(*sem):
    return pltpu.CompilerParams(dimension_semantics=sem, vmem_limit_bytes=VMEM_LIMIT)


def _mod_kernel(c_ref, w_ref, b_ref, o_ref):
    c = c_ref[...]
    a = c * jax.nn.sigmoid(c)
    o_ref[...] = jnp.dot(a, w_ref[...], preferred_element_type=F32,
                         precision=lax.Precision.HIGHEST) + b_ref[...]


def modulation(c, w_mod, b_mod, tn=1024):
    n, d = c.shape
    cols = w_mod.shape[1]
    return pl.pallas_call(
        _mod_kernel,
        grid=(cols // tn,),
        in_specs=[pl.BlockSpec((n, d), lambda j: (0, 0)),
                  pl.BlockSpec((d, tn), lambda j: (0, j)),
                  pl.BlockSpec((1, tn), lambda j: (0, j))],
        out_specs=pl.BlockSpec((n, tn), lambda j: (0, j)),
        out_shape=jax.ShapeDtypeStruct((n, cols), F32),
        compiler_params=_cparams("arbitrary"),
        name="modulation",
    )(c, w_mod, b_mod.reshape(1, cols))


def _norm_mod_kernel(x_ref, g_ref, sc_ref, sh_ref, o_ref, *maybe_ot_ref):
    x = x_ref[...]
    y = x * lax.rsqrt(jnp.mean(x * x, axis=-1, keepdims=True) + EPS) * g_ref[...]
    h = y * (1.0 + sc_ref[...]) + sh_ref[...]
    o_ref[...] = h.astype(o_ref.dtype)
    for ot_ref in maybe_ot_ref:
        ot_ref[...] = h.T.astype(ot_ref.dtype)


def _mod_spec(arr, tm, rows_per_group):
    if arr.ndim == 3:
        bpg = rows_per_group // tm
        return pl.BlockSpec((None, 1, arr.shape[-1]), lambda i, *_: (i // bpg, 0, 0))
    return pl.BlockSpec((tm, arr.shape[-1]), lambda i, *_: (i, 0))


def norm_modulate(x, g, scale, shift, rows_per_group, tm, with_transpose=False):
    n, d = x.shape
    out_specs = [pl.BlockSpec((tm, d), lambda i: (i, 0))]
    out_shape = [jax.ShapeDtypeStruct((n, d), BF16)]
    if with_transpose:
        out_specs.append(pl.BlockSpec((d, tm), lambda i: (0, i)))
        out_shape.append(jax.ShapeDtypeStruct((d, n), BF16))
    return pl.pallas_call(
        _norm_mod_kernel,
        grid=(n // tm,),
        in_specs=[pl.BlockSpec((tm, d), lambda i: (i, 0)),
                  pl.BlockSpec((1, d), lambda i: (0, 0)),
                  _mod_spec(scale, tm, rows_per_group),
                  _mod_spec(shift, tm, rows_per_group)],
        out_specs=out_specs,
        out_shape=out_shape,
        compiler_params=_cparams("arbitrary"),
        name="norm_modulate",
    )(x, g.reshape(1, d), scale, shift)


def _proj_kernel(*refs, norm, rope, act, scale, outs):
    h_ref, w_ref = refs[0], refs[1]
    pos = 2
    g_ref = cos_ref = sin_ref = None
    if norm:
        g_ref = refs[pos]
        pos += 1
    if rope:
        cos_ref, sin_ref = refs[pos], refs[pos + 1]
        pos += 2
    out_refs = refs[pos:]
    acc = jnp.dot(h_ref[...], w_ref[...], preferred_element_type=F32)
    tn = acc.shape[1]
    if norm or rope:
        slabs = []
        for u in range(tn // LANES):
            t = acc[:, u * LANES:(u + 1) * LANES]
            if norm:
                t = t * lax.rsqrt(jnp.mean(t * t, axis=-1, keepdims=True) + EPS) * g_ref[...]
            if rope:
                t = t * cos_ref[...] + pltpu.roll(t, LANES // 2, 1) * sin_ref[...]
            slabs.append(t)
    else:
        if scale != 1.0:
            acc = acc * scale
        if act == "sigmoid":
            acc = jax.nn.sigmoid(acc)
        slabs = None
    for kind, o_ref in zip(outs, out_refs):
        if kind == "head_major":
            for u in range(tn // LANES):
                t = acc[:, u * LANES:(u + 1) * LANES] if slabs is None else slabs[u]
                o_ref[u] = t.astype(o_ref.dtype)
        elif slabs is not None:
            for u, t in enumerate(slabs):
                o_ref[:, u * LANES:(u + 1) * LANES] = t.astype(o_ref.dtype)
        else:
            o_ref[...] = acc.astype(o_ref.dtype)


def project(h, w, col0, ncols, *, outs, norm_g=None, cos=None, sin=None, act="none", scale=1.0,
            tm=1024, tn=512):
    n, k = h.shape
    tn = min(tn, ncols)
    tm = min(tm, n)
    assert col0 % tn == 0 and ncols % tn == 0 and n % tm == 0
    c0 = col0 // tn
    in_specs = [pl.BlockSpec((tm, k), lambda i, j: (i, 0)),
                pl.BlockSpec((k, tn), lambda i, j: (0, j + c0))]
    args = [h, w]
    if norm_g is not None:
        in_specs.append(pl.BlockSpec((1, LANES), lambda i, j: (0, 0)))
        args.append(norm_g.reshape(1, LANES))
    if cos is not None:
        in_specs += [pl.BlockSpec((tm, LANES), lambda i, j: (i, 0))] * 2
        args += [cos, sin]
    out_specs, out_shapes = [], []
    for kind, dt in outs:
        if kind == "head_major":
            out_specs.append(pl.BlockSpec((tn // LANES, tm, LANES), lambda i, j: (j, i, 0)))
            out_shapes.append(jax.ShapeDtypeStruct((ncols // LANES, n, LANES), dt))
        else:
            out_specs.append(pl.BlockSpec((tm, tn), lambda i, j: (i, j)))
            out_shapes.append(jax.ShapeDtypeStruct((n, ncols), dt))
    res = pl.pallas_call(
        functools.partial(_proj_kernel, norm=norm_g is not None, rope=cos is not None, act=act,
                          scale=scale, outs=tuple(kd for kd, _ in outs)),
        grid=(n // tm, ncols // tn),
        in_specs=in_specs,
        out_specs=out_specs,
        out_shape=out_shapes,
        compiler_params=_cparams("arbitrary", "arbitrary"),
        name="project",
    )(*args)
    return res


def _sortable_key(score):
    bits = pltpu.bitcast(score, I32)
    return jnp.where(bits < 0, bits ^ jnp.int32(0x7FFFFFFF), bits)


NEG_INF_KEY = (-8388608) ^ 0x7FFFFFFF


def _lane_group_sum(x, period):
    shift = period
    while shift < LANES:
        x = x + pltpu.roll(x, shift, 1)
        shift *= 2
    return x


def _topk_threshold(keys_ref, nchunks, ck, k, width, max_rows, period=None):
    acc_rows = math.gcd(ck, 64)

    def count(pred):
        def body(c, cnt):
            r0 = pl.multiple_of(c * ck, 8)
            kk = keys_ref[pl.ds(r0, ck), :]
            rows = r0 + lax.broadcasted_iota(I32, kk.shape, 0)
            hit = jnp.where(pred(kk, rows), 1.0, 0.0)
            return cnt + jnp.sum(hit.reshape(ck // acc_rows, acc_rows, width), axis=0)
        part = lax.fori_loop(0, nchunks, body, jnp.zeros((acc_rows, width), F32))
        return jnp.sum(part, axis=0, keepdims=True)

    if period is None:
        def bit_step(i, ut):
            uc = ut | jnp.left_shift(jnp.int32(1), jnp.int32(31) - i)
            cand = uc ^ jnp.int32(INT_MIN)
            return jnp.where(count(lambda kk, _: kk >= cand) >= float(k), uc, ut)
        ut = lax.fori_loop(0, 32, bit_step, jnp.zeros((1, width), I32))
    else:
        replica = lax.broadcasted_iota(I32, (1, width), 1) // period
        bits_per_pass = (width // period).bit_length() - 1
        same_col = jnp.where(lax.broadcasted_iota(I32, (width, width), 0) % period
                             == lax.broadcasted_iota(I32, (width, width), 1) % period, 1.0, 0.0).astype(BF16)
        ut = jnp.zeros((1, width), I32)
        hi = 32
        while hi > 0:
            nb = min(bits_per_pass, hi)
            lo = hi - nb
            cand = (ut | jnp.left_shift(replica, lo)) ^ jnp.int32(INT_MIN)
            ok = jnp.where(count(lambda kk, _, cand=cand: kk >= cand) >= float(k), 1.0, 0.0)
            ok = jnp.where(replica < 2 ** nb, ok, 0.0)
            n_ok = jnp.dot(jnp.broadcast_to(ok, (8, width)).astype(BF16), same_col,
                           preferred_element_type=F32)[0:1]
            best = n_ok.astype(I32) - 1
            ut = ut | jnp.left_shift(best, lo)
            hi = lo
    thr = ut ^ jnp.int32(INT_MIN)

    real = thr > jnp.int32(NEG_INF_KEY)
    surplus = jnp.where(real, count(lambda kk, _: kk >= thr) - float(k), 0.0)

    @pl.when(jnp.max(surplus) > 0.0)
    def _():
        need = float(k) - count(lambda kk, _: kk > thr)
        n_bits = max(1, (max_rows - 1).bit_length())

        def idx_step(i, jv):
            cand = jv | jnp.left_shift(jnp.int32(1), jnp.int32(n_bits - 1) - i)
            below = count(lambda kk, rows: jnp.where(kk == thr, rows, jnp.int32(2 ** 30)) < cand)
            return jnp.where(below < need, cand, jv)
        last = lax.fori_loop(0, n_bits, idx_step, jnp.zeros((1, width), I32))
        last = jnp.where(surplus > 0.0, last, jnp.int32(2 ** 30))

        def demote(c, carry):
            r0 = pl.multiple_of(c * ck, 8)
            kk = keys_ref[pl.ds(r0, ck), :]
            rows = r0 + lax.broadcasted_iota(I32, kk.shape, 0)
            keys_ref[pl.ds(r0, ck), :] = jnp.where(jnp.where(kk == thr, rows, 0) > last, thr - 1, kk)
            return carry
        lax.fori_loop(0, nchunks, demote, 0)

    return jnp.maximum(thr, jnp.int32(NEG_INF_KEY + 1))


def _attn_prompt_kernel(iq_ref, iw_ref, ik_ref, q_ref, k_ref, v_ref, o_ref, keys_ref, *,
                        nq, ck, k_sel, n_idx_heads, n_kv, group, sm_scale):
    j = pl.program_id(1)
    nchunks = (j * nq) // ck + 1
    hd = LANES

    qpos = j * nq + lax.broadcasted_iota(I32, (ck, nq), 1)

    def score_chunk(c, carry):
        r0 = pl.multiple_of(c * ck, ck)
        ikc = ik_ref[pl.ds(r0, ck), :]
        acc = jnp.zeros((ck, nq), F32)
        for hp in range(n_idx_heads // 2):
            rhs = iq_ref[2 * hp:2 * hp + 2].reshape(2 * nq, hd)
            d = lax.dot_general(ikc, rhs, NT_DIMS, preferred_element_type=F32)
            for u in range(2):
                h = 2 * hp + u
                acc = acc + jnp.maximum(d[:, u * nq:(u + 1) * nq], 0.0) * iw_ref[h:h + 1, :]
        kpos = r0 + lax.broadcasted_iota(I32, (ck, nq), 0)
        score = jnp.where(kpos <= qpos, acc, -jnp.inf)
        keys_ref[pl.ds(r0, ck), :] = _sortable_key(score)
        return carry

    lax.fori_loop(0, nchunks, score_chunk, 0)

    thr = _topk_threshold(keys_ref, nchunks, ck, k_sel, nq, keys_ref.shape[0])

    gw = group * nq
    exp2_scale = sm_scale * math.log2(math.e)
    for kvh in range(n_kv):
        qg = q_ref[group * kvh:group * (kvh + 1)].reshape(gw, hd)

        def attend(c, carry):
            m, l, acc = carry
            r0 = pl.multiple_of(c * ck, ck)
            kc = k_ref[pl.ds(r0, ck), kvh * hd:(kvh + 1) * hd]
            s = lax.dot_general(kc, qg, NT_DIMS, preferred_element_type=F32)
            sel = keys_ref[pl.ds(r0, ck), :] >= thr
            s = jnp.concatenate([jnp.where(sel, s[:, g * nq:(g + 1) * nq], NEG_BIG) for g in range(group)], axis=1)
            m_new = jnp.maximum(m, jnp.max(s, axis=0, keepdims=True))
            alpha = jnp.exp2((m - m_new) * exp2_scale)
            p = jnp.exp2((s - m_new) * exp2_scale)
            l = alpha * l + jnp.sum(p, axis=0, keepdims=True)
            vc = v_ref[pl.ds(r0, ck), kvh * hd:(kvh + 1) * hd]
            pv = lax.dot_general(vc, p.astype(BF16), TN_DIMS, preferred_element_type=F32)
            return m_new, l, alpha * acc + pv

        init = (jnp.full((1, gw), NEG_BIG, F32), jnp.zeros((1, gw), F32), jnp.zeros((hd, gw), F32))
        _, l, acc = lax.fori_loop(0, nchunks, attend, init)
        out_t = acc / l
        for g in range(group):
            head = group * kvh + g
            o_ref[:, head * hd:(head + 1) * hd] = out_t[:, g * nq:(g + 1) * nq].T.astype(o_ref.dtype)


def prompt_attention(iq_hm, iw_t, ik, q_hm, k, v, batch, seq, k_sel, nq=128, ck=512):
    n_idx_heads = iq_hm.shape[0]
    n_heads = q_hm.shape[0]
    hd = LANES
    n_kv = k.shape[1] // hd
    group = n_heads // n_kv
    ck = min(ck, seq)
    nqb = seq // nq
    kern = functools.partial(_attn_prompt_kernel, nq=nq, ck=ck, k_sel=k_sel, n_idx_heads=n_idx_heads,
                             n_kv=n_kv, group=group, sm_scale=hd ** -0.5)
    return pl.pallas_call(
        kern,
        grid=(batch, nqb),
        in_specs=[pl.BlockSpec((n_idx_heads, nq, hd), lambda b, j: (0, b * nqb + j, 0)),
                  pl.BlockSpec((n_idx_heads, nq), lambda b, j: (0, b * nqb + j)),
                  pl.BlockSpec((seq, hd), lambda b, j: (b, 0)),
                  pl.BlockSpec((n_heads, nq, hd), lambda b, j: (0, b * nqb + j, 0)),
                  pl.BlockSpec((seq, n_kv * hd), lambda b, j: (b, 0)),
                  pl.BlockSpec((seq, n_kv * hd), lambda b, j: (b, 0))],
        out_specs=pl.BlockSpec((nq, n_heads * hd), lambda b, j: (b * nqb + j, 0)),
        out_shape=jax.ShapeDtypeStruct((batch * seq, n_heads * hd), BF16),
        scratch_shapes=[pltpu.VMEM((seq, nq), I32)],
        compiler_params=_cparams("arbitrary", "arbitrary"),
        name="prompt_attention",
    )(iq_hm, iw_t, ik, q_hm, k, v)


def _attn_sample_kernel(pt_ref, iq_ref, iw_ref, q_ref, ikn_ref, kn_ref, vn_ref, *rest,
                        n_pages, page, t_new, k_sel, sm_scale):
    del pt_ref
    ik_pages = rest[:n_pages]
    k_pages = rest[n_pages:2 * n_pages]
    v_pages = rest[2 * n_pages:3 * n_pages]
    o_ref = rest[3 * n_pages]
    keys_ref, ikb_ref, kb_ref, vb_ref = rest[3 * n_pages + 1:3 * n_pages + 5]
    past = n_pages * page
    new_rows = ikn_ref.shape[0]
    nrows = past + new_rows
    hd = kn_ref.shape[1]
    n_kv = kn_ref.shape[0] // new_rows

    blocks = [(ik_pages[p], k_pages[p], v_pages[p], p * page, page) for p in range(n_pages)]
    blocks.append((ikn_ref, kn_ref, vn_ref, past, new_rows))
    for ik_blk, k_blk, v_blk, r0, rows in blocks:
        ikb_ref[r0:r0 + rows, :] = ik_blk[...].astype(BF16)
        for kvh in range(n_kv):
            kb_ref[r0:r0 + rows, kvh * hd:(kvh + 1) * hd] = k_blk[pl.ds(kvh, rows, stride=n_kv), :].astype(BF16)
            vb_ref[r0:r0 + rows, kvh * hd:(kvh + 1) * hd] = v_blk[pl.ds(kvh, rows, stride=n_kv), :].astype(BF16)

    lane_in = lax.broadcasted_iota(I32, (LANES, LANES), 0) % t_new
    lane_out = lax.broadcasted_iota(I32, (LANES, LANES), 1) % t_new
    head_sum = jnp.where(lane_in == lane_out, 1.0, 0.0).astype(BF16)
    d = lax.dot_general(ikb_ref[...], iq_ref[...], NT_DIMS, preferred_element_type=F32)
    part = jnp.maximum(d, 0.0) * iw_ref[...]
    score = jnp.zeros((nrows, LANES), F32)
    for _ in range(3):
        piece = part.astype(BF16)
        score = score + jnp.dot(piece, head_sum, preferred_element_type=F32)
        part = part - piece.astype(F32)
    new_r = lax.broadcasted_iota(I32, (nrows, LANES), 0) - past
    lane_q = lax.broadcasted_iota(I32, (nrows, LANES), 1) % t_new
    keys_ref[0:nrows, :] = _sortable_key(jnp.where(new_r <= lane_q, score, -jnp.inf))
    keys_ref[nrows:, :] = jnp.full((keys_ref.shape[0] - nrows, LANES), NEG_INF_KEY, I32)

    thr = _topk_threshold(keys_ref, 1, keys_ref.shape[0], k_sel, LANES, keys_ref.shape[0], period=t_new)

    s = lax.dot_general(kb_ref[...], q_ref[...], NT_DIMS, preferred_element_type=F32)
    s = jnp.where(keys_ref[0:nrows, :] >= thr, s * sm_scale, NEG_BIG)
    m = jnp.max(s, axis=0, keepdims=True)
    pr = jnp.exp(s - m).astype(BF16)
    lsum = lax.dot_general(pr, jnp.ones((nrows, LANES), BF16), TN_DIMS, preferred_element_type=F32)
    out = lax.dot_general(pr, vb_ref[...], TN_DIMS, preferred_element_type=F32)
    o_ref[...] = out / jnp.concatenate([lsum] * n_kv, axis=1)


def sample_attention(iq_rows, iw_row, q_blk, ik_new, k_new, v_new, cache_idx_k, cache_k, cache_v,
                     page_table, t_new, k_sel):
    bd, n_pages = page_table.shape
    page = cache_idx_k.shape[1]
    idx_dim = cache_idx_k.shape[2]
    n_kv, hd = cache_k.shape[2], cache_k.shape[3]
    kvw = n_kv * hd
    qw = q_blk.shape[1]
    assert qw == LANES and hd == LANES
    new_rows = ik_new.shape[1]
    nrows = n_pages * page + new_rows
    cache_k = cache_k.reshape(-1, page * n_kv, hd)
    cache_v = cache_v.reshape(-1, page * n_kv, hd)
    k_new = k_new.reshape(bd, new_rows * n_kv, hd)
    v_new = v_new.reshape(bd, new_rows * n_kv, hd)

    def page_spec(rows, p):
        return pl.BlockSpec((None, rows, hd), lambda b, pt: (pt[b * n_pages + p], 0, 0))

    def per_seq(shape):
        return pl.BlockSpec((None,) + shape, lambda b, pt: (b,) + (0,) * len(shape))

    in_specs = [per_seq((LANES, idx_dim)), per_seq((1, LANES)), per_seq((qw, kvw)),
                per_seq((new_rows, idx_dim)), per_seq((new_rows * n_kv, hd)), per_seq((new_rows * n_kv, hd))]
    in_specs += [page_spec(page, p) for p in range(n_pages)]
    in_specs += [page_spec(page * n_kv, p) for p in range(n_pages)]
    in_specs += [page_spec(page * n_kv, p) for p in range(n_pages)]
    kern = functools.partial(_attn_sample_kernel, n_pages=n_pages, page=page, t_new=t_new, k_sel=k_sel,
                             sm_scale=LANES ** -0.5)
    return pl.pallas_call(
        kern,
        grid_spec=pltpu.PrefetchScalarGridSpec(
            num_scalar_prefetch=1,
            grid=(bd,),
            in_specs=in_specs,
            out_specs=pl.BlockSpec((None, qw, kvw), lambda b, pt: (b, 0, 0)),
            scratch_shapes=[pltpu.VMEM((-(-nrows // 64) * 64, LANES), I32),
                            pltpu.VMEM((nrows, idx_dim), BF16),
                            pltpu.VMEM((nrows, kvw), BF16), pltpu.VMEM((nrows, kvw), BF16)]),
        out_shape=jax.ShapeDtypeStruct((bd, qw, kvw), F32),
        compiler_params=_cparams("arbitrary"),
        name="sample_attention",
    )(page_table.reshape(-1), iq_rows, iw_row, q_blk, ik_new, k_new, v_new,
      *([cache_idx_k] * n_pages), *([cache_k] * n_pages), *([cache_v] * n_pages))


def _pool_prompt_kernel(p_ref, halo_ref, w_ref, ps_ref, o_ref, ext_ref, *, tm):
    sblk = pl.program_id(1)
    c = p_ref.shape[-1]
    gc = c // len(POOL_WINDOWS)
    od = o_ref.shape[-1] // len(POOL_WINDOWS)
    ext_ref[0:POOL_HALO, :] = jnp.where(sblk > 0, halo_ref[...], 0.0)
    ext_ref[POOL_HALO:POOL_HALO + tm, :] = p_ref[...]
    pos = sblk * tm + lax.broadcasted_iota(I32, (tm, gc), 0)
    for g, w in enumerate(POOL_WINDOWS):
        cols = slice(g * gc, (g + 1) * gc)
        tot = ext_ref[POOL_HALO:POOL_HALO + tm, cols]
        for dlt in range(1, w):
            tot = tot + ext_ref[POOL_HALO - dlt:POOL_HALO - dlt + tm, cols]
        cnt = jnp.minimum(pos + 1, w).astype(F32)
        pooled = tot / cnt - p_ref[:, cols]
        out = jnp.dot(pooled.astype(BF16), w_ref[g], preferred_element_type=F32)
        o_ref[:, g * od:(g + 1) * od] = (out * ps_ref[:, g * od:(g + 1) * od]).astype(o_ref.dtype)


def pool_prompt(p_in, w_pool, pool_scale, tm=512):
    b, s, c = p_in.shape
    g, gc, od = w_pool.shape
    tm = min(tm, s)
    nsb = s // tm
    hb = tm // POOL_HALO
    return pl.pallas_call(
        functools.partial(_pool_prompt_kernel, tm=tm),
        grid=(b, nsb),
        in_specs=[pl.BlockSpec((None, tm, c), lambda bi, si: (bi, si, 0)),
                  pl.BlockSpec((None, POOL_HALO, c), lambda bi, si: (bi, jnp.maximum(si * hb - 1, 0), 0)),
                  pl.BlockSpec((g, gc, od), lambda bi, si: (0, 0, 0)),
                  pl.BlockSpec((1, g * od), lambda bi, si: (0, 0))],
        out_specs=pl.BlockSpec((tm, g * od), lambda bi, si: (bi * nsb + si, 0)),
        out_shape=jax.ShapeDtypeStruct((b * s, g * od), F32),
        scratch_shapes=[pltpu.VMEM((POOL_HALO + tm, c), F32)],
        compiler_params=_cparams("arbitrary", "arbitrary"),
        name="pool_prompt",
    )(p_in, p_in, w_pool, pool_scale.reshape(1, g * od))


def _pool_sample_kernel(pad_ref, w_ref, ps_ref, o_ref, *, t_new, past):
    n_state = pad_ref.shape[0] - t_new
    c = pad_ref.shape[-1]
    gc = c // len(POOL_WINDOWS)
    od = o_ref.shape[-1] // len(POOL_WINDOWS)
    for t in range(t_new):
        for g, w in enumerate(POOL_WINDOWS):
            cols = slice(g * gc, (g + 1) * gc)
            cur = pad_ref[n_state + t, :, cols]
            tot = cur
            for dlt in range(1, w):
                tot = tot + pad_ref[n_state + t - dlt, :, cols]
            pooled = tot / float(min(past + t + 1, w)) - cur
            out = jnp.dot(pooled.astype(BF16), w_ref[g], preferred_element_type=F32)
            o_ref[t, :, g * od:(g + 1) * od] = out * ps_ref[:, g * od:(g + 1) * od]


def pool_sample(pad_t, w_pool, pool_scale, t_new, past):
    _, bd, c = pad_t.shape
    g, gc, od = w_pool.shape
    return pl.pallas_call(
        functools.partial(_pool_sample_kernel, t_new=t_new, past=past),
        out_shape=jax.ShapeDtypeStruct((t_new, bd, g * od), F32),
        compiler_params=pltpu.CompilerParams(vmem_limit_bytes=VMEM_LIMIT),
        name="pool_sample",
    )(pad_t, w_pool, pool_scale.reshape(1, g * od))


def _merge_kernel(a_ref, w_ref, ga_ref, gb_ref, pool_ref, o_ref):
    acc = jnp.dot(a_ref[...], w_ref[...], preferred_element_type=F32)
    o_ref[...] = (ga_ref[...].astype(F32) * acc + gb_ref[...].astype(F32) * pool_ref[...]).astype(o_ref.dtype)


def merge_branches(attn, w_attn_proj, sig_a, sig_b, pool_out, tm=1024, tn=512):
    n, k = attn.shape
    d = w_attn_proj.shape[1]
    tm, tn = min(tm, n), min(tn, d)
    blk = pl.BlockSpec((tm, tn), lambda i, j: (i, j))
    return pl.pallas_call(
        _merge_kernel,
        grid=(n // tm, d // tn),
        in_specs=[pl.BlockSpec((tm, k), lambda i, j: (i, 0)),
                  pl.BlockSpec((k, tn), lambda i, j: (0, j)), blk, blk, blk],
        out_specs=blk,
        out_shape=jax.ShapeDtypeStruct((n, d), BF16),
        compiler_params=_cparams("arbitrary", "arbitrary"),
        name="merge_branches",
    )(attn, w_attn_proj, sig_a, sig_b, pool_out)


def _out_proj_kernel(m_ref, w_ref, x_ref, gate_ref, o_ref):
    acc = jnp.dot(m_ref[...], w_ref[...], preferred_element_type=F32)
    o_ref[...] = x_ref[...] + gate_ref[...] * acc


def out_project(merged, w_out, x, gate, rows_per_group, tm=1024, tn=512):
    n, k = merged.shape
    d = w_out.shape[1]
    tm, tn = min(tm, n), min(tn, d)
    if gate.ndim == 3:
        tm = min(tm, rows_per_group)
        bpg = rows_per_group // tm
        gate_spec = pl.BlockSpec((None, 1, tn), lambda i, j: (i // bpg, 0, j))
    else:
        gate_spec = pl.BlockSpec((tm, tn), lambda i, j: (i, j))
    blk = pl.BlockSpec((tm, tn), lambda i, j: (i, j))
    return pl.pallas_call(
        _out_proj_kernel,
        grid=(n // tm, d // tn),
        in_specs=[pl.BlockSpec((tm, k), lambda i, j: (i, 0)),
                  pl.BlockSpec((k, tn), lambda i, j: (0, j)), blk, gate_spec],
        out_specs=blk,
        out_shape=jax.ShapeDtypeStruct((n, d), F32),
        compiler_params=_cparams("arbitrary", "arbitrary"),
        name="out_project",
    )(merged, w_out, x, gate)


def _top_lists(s, n_top):
    rows, t = s.shape
    iota = lax.broadcasted_iota(I32, (rows, t), 0).astype(F32)
    rank = jnp.full((rows, t), float(rows), F32)
    work = s
    vals = []
    for r in range(n_top):
        m = jnp.max(work, axis=0, keepdims=True)
        idx = jnp.min(jnp.where(work == m, iota, float(rows)), axis=0, keepdims=True)
        hit = iota == idx
        work = jnp.where(hit, -jnp.inf, work)
        rank = jnp.where(hit, float(r), rank)
        vals.append(m)
    return jnp.concatenate(vals, axis=0), rank


def _peer_route_kernel(pq_ref, keys_ref, r2_ref, e2_ref, n1_ref, e1_ref, *, n_heads, n_top):
    t = pq_ref.shape[1]
    pieces = []
    for a in range(n_top):
        nb = n_top // (a + 1)
        rows = -(-nb // 8) * 8
        pieces.append((a, nb, rows))
    flat_idx = jnp.concatenate(
        [a * n_top + lax.broadcasted_iota(I32, (rows, t), 0) for a, _, rows in pieces], axis=0).astype(F32)
    n_cand = flat_idx.shape[0]

    def route_head(h, carry):
        s1 = lax.dot_general(keys_ref[2 * h], pq_ref[2 * h], NT_DIMS, preferred_element_type=F32)
        s2 = lax.dot_general(keys_ref[2 * h + 1], pq_ref[2 * h + 1], NT_DIMS, preferred_element_type=F32)
        v1, rank1 = _top_lists(s1, n_top)
        v2, rank2 = _top_lists(s2, n_top)
        cand = jnp.concatenate(
            [jnp.where(lax.broadcasted_iota(I32, (rows, t), 0) < nb, v1[a:a + 1] + v2[:rows], -jnp.inf)
             for a, nb, rows in pieces], axis=0)
        chosen = jnp.zeros((n_cand, t), F32)
        top = cand[0:1]
        z = jnp.zeros((1, t), F32)
        for _ in range(n_top):
            m = jnp.max(cand, axis=0, keepdims=True)
            idx = jnp.min(jnp.where(cand == m, flat_idx, float(n_top * n_top)), axis=0, keepdims=True)
            hit = flat_idx == idx
            cand = jnp.where(hit, -jnp.inf, cand)
            chosen = jnp.where(hit, 1.0, chosen)
            z = z + jnp.exp(m - top)
        n1 = jnp.zeros_like(rank1)
        r0 = 0
        for a, _, rows in pieces:
            n_a = jnp.sum(chosen[r0:r0 + rows], axis=0, keepdims=True)
            n1 = jnp.where(rank1 == float(a), n_a, n1)
            r0 += rows
        inv_z = 1.0 / z
        r2_ref[h] = rank2.astype(r2_ref.dtype)
        e2_ref[h] = jnp.where(rank2 < float(n_top), jnp.exp(s2 - v2[0:1]), 0.0).astype(e2_ref.dtype)
        n1_ref[h] = n1
        e1_ref[h] = jnp.where(rank1 < float(n_top), jnp.exp(s1 - v1[0:1]), 0.0) * inv_z
        return carry

    lax.fori_loop(0, n_heads, route_head, 0)


def peer_route(pq_hm, keys_flat, n_heads, tp=256):
    n = pq_hm.shape[1]
    nk, half = keys_flat.shape[1], keys_flat.shape[2]
    tp = min(tp, n)
    out_spec = pl.BlockSpec((n_heads, nk, tp), lambda i: (0, 0, i))
    return pl.pallas_call(
        functools.partial(_peer_route_kernel, n_heads=n_heads, n_top=PEER_TOPK),
        grid=(n // tp,),
        in_specs=[pl.BlockSpec((2 * n_heads, tp, half), lambda i: (0, i, 0)),
                  pl.BlockSpec(keys_flat.shape, lambda i: (0, 0, 0))],
        out_specs=[out_spec] * 4,
        out_shape=[jax.ShapeDtypeStruct((n_heads, nk, n), dt) for dt in (BF16, BF16, F32, F32)],
        compiler_params=_cparams("arbitrary"),
        name="peer_route",
    )(pq_hm, keys_flat)


def _peer_expert_kernel(ht_ref, u_ref, vt_ref, r2_ref, e2_ref, n1_ref, e1_ref, x_ref, gate_ref, o_ref,
                        a_ref, w_ref, acc_ref, n1s_ref, e1s_ref, *, n_heads, nk):
    e = pl.program_id(1)
    eb, tm = a_ref.shape
    n_sub = eb // nk

    @pl.when(e == 0)
    def _():
        acc_ref[...] = jnp.zeros_like(acc_ref)

    a_ref[...] = jnp.dot(u_ref[...], ht_ref[...], preferred_element_type=F32)

    for h in range(n_heads):
        for s in range(n_sub):
            n1s_ref[h, s] = jnp.broadcast_to(n1_ref[h, s:s + 1, :], (ROW_BLOCK, tm)).astype(BF16)
            e1s_ref[h, s] = jnp.broadcast_to(e1_ref[h, s:s + 1, :], (ROW_BLOCK, tm)).astype(BF16)

    def sub_block(s, carry):
        r0 = pl.multiple_of(s * nk, nk)
        for rb in range(nk // ROW_BLOCK):
            rr = slice(rb * ROW_BLOCK, (rb + 1) * ROW_BLOCK)
            gsum = jnp.zeros((ROW_BLOCK, tm), BF16)
            for h in range(n_heads):
                keep = r2_ref[h, rr, :] < n1s_ref[h, s]
                gsum = gsum + jnp.where(keep, e2_ref[h, rr, :], jnp.zeros((), BF16)) * e1s_ref[h, s]
            a = a_ref[pl.ds(r0 + rb * ROW_BLOCK, ROW_BLOCK), :]
            act = 0.5 * a * (1.0 + lax.erf(a * (2.0 ** -0.5)))
            w_ref[pl.ds(r0 + rb * ROW_BLOCK, ROW_BLOCK), :] = gsum * act.astype(BF16)
        return carry

    lax.fori_loop(0, n_sub, sub_block, 0)
    acc_ref[...] += jnp.dot(vt_ref[...], w_ref[...], preferred_element_type=F32)

    @pl.when(e == pl.num_programs(1) - 1)
    def _():
        o_ref[...] = x_ref[...] + gate_ref[...] * acc_ref[...].T


def peer_experts(h2_t, u, v_t, r2, e2, n1, e1, x, gate, rows_per_group, tm=512, eb=1024):
    d, n = h2_t.shape
    n_exp = u.shape[0]
    n_heads, nk = r2.shape[0], r2.shape[1]
    tm, eb = min(tm, n), min(eb, n_exp)
    assert tm % LANES == 0 and eb % nk == 0 and (eb // nk) % 8 == 0 and nk % ROW_BLOCK == 0
    n_sub = eb // nk
    tok_tile = pl.BlockSpec((n_heads, nk, tm), lambda i, e: (0, 0, i))
    i1_tile = pl.BlockSpec((n_heads, n_sub, tm), lambda i, e: (0, e, i))
    return pl.pallas_call(
        functools.partial(_peer_expert_kernel, n_heads=n_heads, nk=nk),
        grid=(n // tm, n_exp // eb),
        in_specs=[pl.BlockSpec((d, tm), lambda i, e: (0, i)),
                  pl.BlockSpec((eb, d), lambda i, e: (e, 0)),
                  pl.BlockSpec((d, eb), lambda i, e: (0, e)),
                  tok_tile, tok_tile, i1_tile, i1_tile,
                  pl.BlockSpec((tm, d), lambda i, e: (i, 0)),
                  _mod_spec(gate, tm, rows_per_group)],
        out_specs=pl.BlockSpec((tm, d), lambda i, e: (i, 0)),
        out_shape=jax.ShapeDtypeStruct((n, d), F32),
        scratch_shapes=[pltpu.VMEM((eb, tm), F32), pltpu.VMEM((eb, tm), BF16), pltpu.VMEM((d, tm), F32),
                        pltpu.VMEM((n_heads, n_sub, ROW_BLOCK, tm), BF16),
                        pltpu.VMEM((n_heads, n_sub, ROW_BLOCK, tm), BF16)],
        compiler_params=_cparams("arbitrary", "arbitrary"),
        name="peer_experts",
    )(h2_t, u, v_t, r2, e2, n1, e1, x, gate)


def _rope_tables(pos, half):
    freqs = ROPE_THETA ** (-jnp.arange(half, dtype=F32) / half)
    ang = pos.astype(F32)[:, None] * freqs[None, :]
    cos, sin = jnp.cos(ang), jnp.sin(ang)
    return jnp.concatenate([cos, cos], axis=1), jnp.concatenate([-sin, sin], axis=1)


def _channel_mixer(x_mid, shift2, scale2, gate2, rows_per_group, norm2_g, w_peer_q, keys_flat, peer_u, peer_vt,
                   n_peer_heads, tm):
    h2, h2_t = norm_modulate(x_mid, norm2_g, scale2, shift2, rows_per_group, tm, with_transpose=True)
    (pq_hm,) = project(h2, w_peer_q, 0, w_peer_q.shape[1], outs=(("head_major", BF16),))
    r2, e2, n1, e1 = peer_route(pq_hm, keys_flat, n_peer_heads)
    return peer_experts(h2_t, peer_u, peer_vt, r2, e2, n1, e1, x_mid, gate2, rows_per_group, tm=tm)


def kernel(x_prompt, x_sample, cache_k, cache_v, cache_idx_k, state_pool, page_table, c_prompt, c_sample,
           norm1_g, norm2_g, w_mod, b_mod, w_in, q_norm_g, k_norm_g, w_attn_proj, w_pool, pool_scale,
           w_out, w_peer_q, peer_keys, peer_u, peer_v):
    b, s, d = x_prompt.shape
    bd, t_new, _ = x_sample.shape
    hd = q_norm_g.shape[0]
    idx_dim = cache_idx_k.shape[2]
    page = cache_k.shape[1]
    n_kv = cache_k.shape[2]
    n_heads = w_attn_proj.shape[0] // hd
    d_pool = state_pool.shape[2]
    n_state = state_pool.shape[1]
    q_cols, kv_cols = n_heads * hd, n_kv * hd
    idx_heads = (w_in.shape[1] - q_cols - 2 * kv_cols - idx_dim - d_pool - 2 * d) // (idx_dim + 1)
    iq_cols = idx_heads * idx_dim
    past = page_table.shape[1] * page
    n_peer_heads, _, nk, half = peer_keys.shape
    assert hd == LANES and idx_dim == LANES and nk == LANES and half == LANES

    o_ik = q_cols + 2 * kv_cols + iq_cols
    o_iw = o_ik + idx_dim
    iw_pad = jnp.zeros((d, LANES - idx_heads), w_in.dtype)
    w_in_b = jnp.concatenate([w_in[:, :o_ik], w_in[:, o_iw + idx_heads:], w_in[:, o_ik:o_iw],
                              w_in[:, o_iw:o_iw + idx_heads], iw_pad], axis=1).astype(BF16)
    c_q, c_k, c_v = 0, q_cols, q_cols + kv_cols
    c_iq = c_v + kv_cols
    c_pool = c_iq + iq_cols
    c_ga = c_pool + d_pool
    c_gb = c_ga + d
    c_ik = c_gb + d
    c_iw = c_ik + idx_dim
    w_ap_b = w_attn_proj.astype(BF16)
    w_out_b = w_out.astype(BF16)
    w_pq_b = w_peer_q.astype(BF16)
    w_pool_b = w_pool.astype(BF16)
    keys_flat = peer_keys.reshape(n_peer_heads * 2, nk, half).astype(BF16)
    peer_u_b = peer_u.astype(BF16)
    peer_vt_b = peer_v.astype(BF16).T
    iw_scale = idx_heads ** -0.5 * idx_dim ** -0.5

    mod = modulation(jnp.concatenate([c_prompt, c_sample], axis=0), w_mod, b_mod)
    mod = mod.reshape(b + bd, 6, d)
    mod_p = [mod[:b, i][:, None, :] for i in range(6)]
    mod_s = [jnp.repeat(mod[b:, i], t_new, axis=0) for i in range(6)]

    def mixer_inputs(x2, shift, scale, pos, rows_per_group, tm):
        cos, sin = (jnp.tile(tbl, (x2.shape[0] // pos.shape[0], 1)) for tbl in _rope_tables(pos, hd // 2))
        (h,) = norm_modulate(x2, norm1_g, scale, shift, rows_per_group, tm)
        (q_hm,) = project(h, w_in_b, c_q, q_cols, outs=(("head_major", BF16),), norm_g=q_norm_g, cos=cos, sin=sin, tm=tm)
        k_f, k_b = project(h, w_in_b, c_k, kv_cols, outs=(("token_major", F32), ("token_major", BF16)),
                           norm_g=k_norm_g, cos=cos, sin=sin, tm=tm)
        v_f, v_b = project(h, w_in_b, c_v, kv_cols, outs=(("token_major", F32), ("token_major", BF16)), tm=tm)
        (iq_hm,) = project(h, w_in_b, c_iq, iq_cols, outs=(("head_major", BF16),), cos=cos, sin=sin, tm=tm)
        ik_f, ik_b = project(h, w_in_b, c_ik, idx_dim, outs=(("token_major", F32), ("token_major", BF16)),
                             cos=cos, sin=sin, tm=tm)
        (iw,) = project(h, w_in_b, c_iw, LANES, outs=(("token_major", F32),), scale=iw_scale, tm=tm)
        (p_in,) = project(h, w_in_b, c_pool, d_pool, outs=(("token_major", F32),), tm=tm)
        (sig_a,) = project(h, w_in_b, c_ga, d, outs=(("token_major", BF16),), act="sigmoid", tm=tm)
        (sig_b,) = project(h, w_in_b, c_gb, d, outs=(("token_major", BF16),), act="sigmoid", tm=tm)
        return q_hm, k_f, k_b, v_f, v_b, iq_hm, ik_f, ik_b, iw[:, :idx_heads], p_in, sig_a, sig_b

    xp = x_prompt.reshape(b * s, d)
    tm_p = min(1024, s)
    pos_p = jnp.arange(s)
    (q_hm, k_f, k_b, v_f, v_b, iq_hm, ik_f, ik_b, iw, p_in, sig_a, sig_b) = mixer_inputs(
        xp, mod_p[0], mod_p[1], pos_p, s, tm_p)
    attn_p = prompt_attention(iq_hm, iw.T, ik_b, q_hm, k_b, v_b, b, s, min(TOPK_MAX, s // 4))
    pool_p = pool_prompt(p_in.reshape(b, s, d_pool), w_pool_b, pool_scale)
    merged = merge_branches(attn_p, w_ap_b, sig_a, sig_b, pool_p)
    h_p = out_project(merged, w_out_b, xp, mod_p[2], s)
    y_prompt = _channel_mixer(h_p, mod_p[3], mod_p[4], mod_p[5], s, norm2_g, w_pq_b, keys_flat, peer_u_b,
                              peer_vt_b, n_peer_heads, min(512, s))
    k_prompt = k_f.reshape(b, s, n_kv, hd)
    v_prompt = v_f.reshape(b, s, n_kv, hd)
    idx_k_prompt = ik_f.reshape(b, s, idx_dim)
    pool_state_prompt = p_in.reshape(b, s, d_pool)[:, s - n_state:, :]

    ns = bd * t_new
    xs = x_sample.reshape(ns, d)
    pos_s = past + jnp.arange(t_new)
    (q_hm, k_f, k_b, v_f, v_b, iq_hm, ik_f, ik_b, iw, p_in, sig_a, sig_b) = mixer_inputs(
        xs, mod_s[0], mod_s[1], pos_s, t_new, ns)
    group = n_heads // n_kv
    iq_rows = iq_hm.reshape(idx_heads, bd, t_new, idx_dim).transpose(1, 0, 2, 3).reshape(bd, idx_heads * t_new, idx_dim)
    iq_rows = jnp.pad(iq_rows, ((0, 0), (0, LANES - idx_heads * t_new), (0, 0)))
    iw_row = iw.reshape(bd, t_new, idx_heads).transpose(0, 2, 1).reshape(bd, 1, idx_heads * t_new)
    iw_row = jnp.pad(iw_row, ((0, 0), (0, 0), (0, LANES - idx_heads * t_new)))
    q5 = q_hm.reshape(n_kv, group, bd, t_new, hd).transpose(2, 0, 1, 3, 4)
    eye = jnp.eye(n_kv, dtype=q5.dtype)
    q_blk = (q5[:, :, :, :, None, :] * eye[None, :, None, None, :, None]).reshape(
        bd, n_kv * group * t_new, n_kv * hd)
    q_blk = jnp.pad(q_blk, ((0, 0), (0, LANES - n_kv * group * t_new), (0, 0)))
    new_rows = 16
    def pad_new(a, *tail):
        a = a.reshape((bd, t_new) + tail)
        return jnp.pad(a, ((0, 0), (0, new_rows - t_new)) + ((0, 0),) * len(tail))

    att = sample_attention(iq_rows, iw_row, q_blk, pad_new(ik_f, idx_dim), pad_new(k_f, n_kv, hd),
                           pad_new(v_f, n_kv, hd), cache_idx_k, cache_k, cache_v,
                           page_table, t_new, min(TOPK_MAX, (past + t_new) // 4))
    att = att[:, :n_kv * group * t_new].reshape(bd, n_kv, group, t_new, n_kv, hd)
    att = jnp.einsum('bkgqkd->bqkgd', att).reshape(ns, q_cols).astype(BF16)
    pad_t = jnp.concatenate([state_pool, p_in.reshape(bd, t_new, d_pool)], axis=1)
    pool_s = pool_sample(pad_t.transpose(1, 0, 2), w_pool_b, pool_scale, t_new, past)
    pool_s = pool_s.transpose(1, 0, 2).reshape(ns, d)
    merged = merge_branches(att, w_ap_b, sig_a, sig_b, pool_s)
    h_s = out_project(merged, w_out_b, xs, mod_s[2], t_new)
    y_sample = _channel_mixer(h_s, mod_s[3], mod_s[4], mod_s[5], t_new, norm2_g, w_pq_b, keys_flat, peer_u_b,
                              peer_vt_b, n_peer_heads, ns)

    return (y_prompt.reshape(b, s, d), y_sample.reshape(bd, t_new, d), k_prompt, v_prompt, idx_k_prompt,
            pool_state_prompt, k_f.reshape(bd, t_new, n_kv, hd), v_f.reshape(bd, t_new, n_kv, hd),
            ik_f.reshape(bd, t_new, idx_dim), pad_t[:, t_new:, :])
```

```python
import functools
import math

import jax
import jax.numpy as jnp
from jax import lax
from jax.experimental import pallas as pl
from jax.experimental.pallas import tpu as pltpu

F32 = jnp.float32
BF16 = jnp.bfloat16
I32 = jnp.int32

LANES = 128
EPS = 1e-6
ROPE_THETA = 10000.0
TOPK_MAX = 256
POOL_WINDOWS = (2, 4, 8, 16)
POOL_HALO = 16
PEER_TOPK = 16
ROW_BLOCK = 16
VMEM_LIMIT = 56 * 1024 * 1024

NT_DIMS = (((1,), (1,)), ((), ()))
TN_DIMS = (((0,), (0,)), ((), ()))
NEG_BIG = -1e30
INT_MIN = -2 ** 31


def _cparams(*sem):
    return pltpu.CompilerParams(dimension_semantics=sem, vmem_limit_bytes=VMEM_LIMIT)


def _mod_kernel(c_ref, w_ref, b_ref, o_ref):
    c = c_ref[...]
    a = c * jax.nn.sigmoid(c)
    o_ref[...] = jnp.dot(a, w_ref[...], preferred_element_type=F32,
                         precision=lax.Precision.HIGHEST) + b_ref[...]


def modulation(c, w_mod, b_mod, tn=1024):
    n, d = c.shape
    cols = w_mod.shape[1]
    return pl.pallas_call(
        _mod_kernel,
        grid=(cols // tn,),
        in_specs=[pl.BlockSpec((n, d), lambda j: (0, 0)),
                  pl.BlockSpec((d, tn), lambda j: (0, j)),
                  pl.BlockSpec((1, tn), lambda j: (0, j))],
        out_specs=pl.BlockSpec((n, tn), lambda j: (0, j)),
        out_shape=jax.ShapeDtypeStruct((n, cols), F32),
        compiler_params=_cparams("arbitrary"),
        name="modulation",
    )(c, w_mod, b_mod.reshape(1, cols))


def _norm_mod_kernel(x_ref, g_ref, sc_ref, sh_ref, o_ref, *maybe_ot_ref):
    x = x_ref[...]
    y = x * lax.rsqrt(jnp.mean(x * x, axis=-1, keepdims=True) + EPS) * g_ref[...]
    h = y * (1.0 + sc_ref[...]) + sh_ref[...]
    o_ref[...] = h.astype(o_ref.dtype)
    for ot_ref in maybe_ot_ref:
        ot_ref[...] = h.T.astype(ot_ref.dtype)


def _mod_spec(arr, tm, rows_per_group):
    if arr.ndim == 3:
        bpg = rows_per_group // tm
        return pl.BlockSpec((None, 1, arr.shape[-1]), lambda i, *_: (i // bpg, 0, 0))
    return pl.BlockSpec((tm, arr.shape[-1]), lambda i, *_: (i, 0))


def norm_modulate(x, g, scale, shift, rows_per_group, tm, with_transpose=False):
    n, d = x.shape
    out_specs = [pl.BlockSpec((tm, d), lambda i: (i, 0))]
    out_shape = [jax.ShapeDtypeStruct((n, d), BF16)]
    if with_transpose:
        out_specs.append(pl.BlockSpec((d, tm), lambda i: (0, i)))
        out_shape.append(jax.ShapeDtypeStruct((d, n), BF16))
    return pl.pallas_call(
        _norm_mod_kernel,
        grid=(n // tm,),
        in_specs=[pl.BlockSpec((tm, d), lambda i: (i, 0)),
                  pl.BlockSpec((1, d), lambda i: (0, 0)),
                  _mod_spec(scale, tm, rows_per_group),
                  _mod_spec(shift, tm, rows_per_group)],
        out_specs=out_specs,
        out_shape=out_shape,
        compiler_params=_cparams("arbitrary"),
        name="norm_modulate",
    )(x, g.reshape(1, d), scale, shift)


def _proj_kernel(*refs, norm, rope, act, scale, outs):
    h_ref, w_ref = refs[0], refs[1]
    pos = 2
    g_ref = cos_ref = sin_ref = None
    if norm:
        g_ref = refs[pos]
        pos += 1
    if rope:
        cos_ref, sin_ref = refs[pos], refs[pos + 1]
        pos += 2
    out_refs = refs[pos:]
    acc = jnp.dot(h_ref[...], w_ref[...], preferred_element_type=F32)
    tn = acc.shape[1]
    if norm or rope:
        slabs = []
        for u in range(tn // LANES):
            t = acc[:, u * LANES:(u + 1) * LANES]
            if norm:
                t = t * lax.rsqrt(jnp.mean(t * t, axis=-1, keepdims=True) + EPS) * g_ref[...]
            if rope:
                t = t * cos_ref[...] + pltpu.roll(t, LANES // 2, 1) * sin_ref[...]
            slabs.append(t)
    else:
        if scale != 1.0:
            acc = acc * scale
        if act == "sigmoid":
            acc = jax.nn.sigmoid(acc)
        slabs = None
    for kind, o_ref in zip(outs, out_refs):
        if kind == "head_major":
            for u in range(tn // LANES):
                t = acc[:, u * LANES:(u + 1) * LANES] if slabs is None else slabs[u]
                o_ref[u] = t.astype(o_ref.dtype)
        elif slabs is not None:
            for u, t in enumerate(slabs):
                o_ref[:, u * LANES:(u + 1) * LANES] = t.astype(o_ref.dtype)
        else:
            o_ref[...] = acc.astype(o_ref.dtype)


def project(h, w, col0, ncols, *, outs, norm_g=None, cos=None, sin=None, act="none", scale=1.0,
            tm=1024, tn=512):
    n, k = h.shape
    tn = min(tn, ncols)
    tm = min(tm, n)
    assert col0 % tn == 0 and ncols % tn == 0 and n % tm == 0
    c0 = col0 // tn
    in_specs = [pl.BlockSpec((tm, k), lambda i, j: (i, 0)),
                pl.BlockSpec((k, tn), lambda i, j: (0, j + c0))]
    args = [h, w]
    if norm_g is not None:
        in_specs.append(pl.BlockSpec((1, LANES), lambda i, j: (0, 0)))
        args.append(norm_g.reshape(1, LANES))
    if cos is not None:
        in_specs += [pl.BlockSpec((tm, LANES), lambda i, j: (i, 0))] * 2
        args += [cos, sin]
    out_specs, out_shapes = [], []
    for kind, dt in outs:
        if kind == "head_major":
            out_specs.append(pl.BlockSpec((tn // LANES, tm, LANES), lambda i, j: (j, i, 0)))
            out_shapes.append(jax.ShapeDtypeStruct((ncols // LANES, n, LANES), dt))
        else:
            out_specs.append(pl.BlockSpec((tm, tn), lambda i, j: (i, j)))
            out_shapes.append(jax.ShapeDtypeStruct((n, ncols), dt))
    res = pl.pallas_call(
        functools.partial(_proj_kernel, norm=norm_g is not None, rope=cos is not None, act=act,
                          scale=scale, outs=tuple(kd for kd, _ in outs)),
        grid=(n // tm, ncols // tn),
        in_specs=in_specs,
        out_specs=out_specs,
        out_shape=out_shapes,
        compiler_params=_cparams("arbitrary", "arbitrary"),
        name="project",
    )(*args)
    return res


def _sortable_key(score):
    bits = pltpu.bitcast(score, I32)
    return jnp.where(bits < 0, bits ^ jnp.int32(0x7FFFFFFF), bits)


NEG_INF_KEY = (-8388608) ^ 0x7FFFFFFF


def _lane_group_sum(x, period):
    shift = period
    while shift < LANES:
        x = x + pltpu.roll(x, shift, 1)
        shift *= 2
    return x


def _topk_threshold(keys_ref, nchunks, ck, k, width, max_rows, period=None):
    acc_rows = math.gcd(ck, 64)

    def count(pred):
        def body(c, cnt):
            r0 = pl.multiple_of(c * ck, 8)
            kk = keys_ref[pl.ds(r0, ck), :]
            rows = r0 + lax.broadcasted_iota(I32, kk.shape, 0)
            hit = jnp.where(pred(kk, rows), 1.0, 0.0)
            return cnt + jnp.sum(hit.reshape(ck // acc_rows, acc_rows, width), axis=0)
        part = lax.fori_loop(0, nchunks, body, jnp.zeros((acc_rows, width), F32))
        return jnp.sum(part, axis=0, keepdims=True)

    if period is None:
        def bit_step(i, ut):
            uc = ut | jnp.left_shift(jnp.int32(1), jnp.int32(31) - i)
            cand = uc ^ jnp.int32(INT_MIN)
            return jnp.where(count(lambda kk, _: kk >= cand) >= float(k), uc, ut)
        ut = lax.fori_loop(0, 32, bit_step, jnp.zeros((1, width), I32))
    else:
        replica = lax.broadcasted_iota(I32, (1, width), 1) // period
        bits_per_pass = (width // period).bit_length() - 1
        same_col = jnp.where(lax.broadcasted_iota(I32, (width, width), 0) % period
                             == lax.broadcasted_iota(I32, (width, width), 1) % period, 1.0, 0.0).astype(BF16)
        ut = jnp.zeros((1, width), I32)
        hi = 32
        while hi > 0:
            nb = min(bits_per_pass, hi)
            lo = hi - nb
            cand = (ut | jnp.left_shift(replica, lo)) ^ jnp.int32(INT_MIN)
            ok = jnp.where(count(lambda kk, _, cand=cand: kk >= cand) >= float(k), 1.0, 0.0)
            ok = jnp.where(replica < 2 ** nb, ok, 0.0)
            n_ok = jnp.dot(jnp.broadcast_to(ok, (8, width)).astype(BF16), same_col,
                           preferred_element_type=F32)[0:1]
            best = n_ok.astype(I32) - 1
            ut = ut | jnp.left_shift(best, lo)
            hi = lo
    thr = ut ^ jnp.int32(INT_MIN)

    real = thr > jnp.int32(NEG_INF_KEY)
    surplus = jnp.where(real, count(lambda kk, _: kk >= thr) - float(k), 0.0)

    @pl.when(jnp.max(surplus) > 0.0)
    def _():
        need = float(k) - count(lambda kk, _: kk > thr)
        n_bits = max(1, (max_rows - 1).bit_length())

        def idx_step(i, jv):
            cand = jv | jnp.left_shift(jnp.int32(1), jnp.int32(n_bits - 1) - i)
            below = count(lambda kk, rows: jnp.where(kk == thr, rows, jnp.int32(2 ** 30)) < cand)
            return jnp.where(below < need, cand, jv)
        last = lax.fori_loop(0, n_bits, idx_step, jnp.zeros((1, width), I32))
        last = jnp.where(surplus > 0.0, last, jnp.int32(2 ** 30))

        def demote(c, carry):
            r0 = pl.multiple_of(c * ck, 8)
            kk = keys_ref[pl.ds(r0, ck), :]
            rows = r0 + lax.broadcasted_iota(I32, kk.shape, 0)
            keys_ref[pl.ds(r0, ck), :] = jnp.where(jnp.where(kk == thr, rows, 0) > last, thr - 1, kk)
            return carry
        lax.fori_loop(0, nchunks, demote, 0)

    return jnp.maximum(thr, jnp.int32(NEG_INF_KEY + 1))


def _attn_prompt_kernel(iq_ref, iw_ref, ik_ref, q_ref, k_ref, v_ref, o_ref, keys_ref, *,
                        nq, ck, k_sel, n_idx_heads, n_kv, group, sm_scale):
    j = pl.program_id(1)
    nchunks = (j * nq) // ck + 1
    hd = LANES

    qpos = j * nq + lax.broadcasted_iota(I32, (ck, nq), 1)

    def score_chunk(c, carry):
        r0 = pl.multiple_of(c * ck, ck)
        ikc = ik_ref[pl.ds(r0, ck), :]
        acc = jnp.zeros((ck, nq), F32)
        for hp in range(n_idx_heads // 2):
            rhs = iq_ref[2 * hp:2 * hp + 2].reshape(2 * nq, hd)
            d = lax.dot_general(ikc, rhs, NT_DIMS, preferred_element_type=F32)
            for u in range(2):
                h = 2 * hp + u
                acc = acc + jnp.maximum(d[:, u * nq:(u + 1) * nq], 0.0) * iw_ref[h:h + 1, :]
        kpos = r0 + lax.broadcasted_iota(I32, (ck, nq), 0)
        score = jnp.where(kpos <= qpos, acc, -jnp.inf)
        keys_ref[pl.ds(r0, ck), :] = _sortable_key(score)
        return carry

    lax.fori_loop(0, nchunks, score_chunk, 0)

    thr = _topk_threshold(keys_ref, nchunks, ck, k_sel, nq, keys_ref.shape[0])

    gw = group * nq
    exp2_scale = sm_scale * math.log2(math.e)
    for kvh in range(n_kv):
        qg = q_ref[group * kvh:group * (kvh + 1)].reshape(gw, hd)

        def attend(c, carry):
            m, l, acc = carry
            r0 = pl.multiple_of(c * ck, ck)
            kc = k_ref[pl.ds(r0, ck), kvh * hd:(kvh + 1) * hd]
            s = lax.dot_general(kc, qg, NT_DIMS, preferred_element_type=F32)
            sel = keys_ref[pl.ds(r0, ck), :] >= thr
            s = jnp.concatenate([jnp.where(sel, s[:, g * nq:(g + 1) * nq], NEG_BIG) for g in range(group)], axis=1)
            m_new = jnp.maximum(m, jnp.max(s, axis=0, keepdims=True))
            alpha = jnp.exp2((m - m_new) * exp2_scale)
            p = jnp.exp2((s - m_new) * exp2_scale)
            l = alpha * l + jnp.sum(p, axis=0, keepdims=True)
            vc = v_ref[pl.ds(r0, ck), kvh * hd:(kvh + 1) * hd]
            pv = lax.dot_general(vc, p.astype(BF16), TN_DIMS, preferred_element_type=F32)
            return m_new, l, alpha * acc + pv

        init = (jnp.full((1, gw), NEG_BIG, F32), jnp.zeros((1, gw), F32), jnp.zeros((hd, gw), F32))
        _, l, acc = lax.fori_loop(0, nchunks, attend, init)
        out_t = acc / l
        for g in range(group):
            head = group * kvh + g
            o_ref[:, head * hd:(head + 1) * hd] = out_t[:, g * nq:(g + 1) * nq].T.astype(o_ref.dtype)


def prompt_attention(iq_hm, iw_t, ik, q_hm, k, v, batch, seq, k_sel, nq=128, ck=512):
    n_idx_heads = iq_hm.shape[0]
    n_heads = q_hm.shape[0]
    hd = LANES
    n_kv = k.shape[1] // hd
    group = n_heads // n_kv
    ck = min(ck, seq)
    nqb = seq // nq
    kern = functools.partial(_attn_prompt_kernel, nq=nq, ck=ck, k_sel=k_sel, n_idx_heads=n_idx_heads,
                             n_kv=n_kv, group=group, sm_scale=hd ** -0.5)
    return pl.pallas_call(
        kern,
        grid=(batch, nqb),
        in_specs=[pl.BlockSpec((n_idx_heads, nq, hd), lambda b, j: (0, b * nqb + j, 0)),
                  pl.BlockSpec((n_idx_heads, nq), lambda b, j: (0, b * nqb + j)),
                  pl.BlockSpec((seq, hd), lambda b, j: (b, 0)),
                  pl.BlockSpec((n_heads, nq, hd), lambda b, j: (0, b * nqb + j, 0)),
                  pl.BlockSpec((seq, n_kv * hd), lambda b, j: (b, 0)),
                  pl.BlockSpec((seq, n_kv * hd), lambda b, j: (b, 0))],
        out_specs=pl.BlockSpec((nq, n_heads * hd), lambda b, j: (b * nqb + j, 0)),
        out_shape=jax.ShapeDtypeStruct((batch * seq, n_heads * hd), BF16),
        scratch_shapes=[pltpu.VMEM((seq, nq), I32)],
        compiler_params=_cparams("arbitrary", "arbitrary"),
        name="prompt_attention",
    )(iq_hm, iw_t, ik, q_hm, k, v)


def _attn_sample_kernel(pt_ref, iq_ref, iw_ref, q_ref, ikn_ref, kn_ref, vn_ref, *rest,
                        n_pages, page, t_new, k_sel, sm_scale):
    del pt_ref
    ik_pages = rest[:n_pages]
    k_pages = rest[n_pages:2 * n_pages]
    v_pages = rest[2 * n_pages:3 * n_pages]
    o_ref = rest[3 * n_pages]
    keys_ref, ikb_ref, kb_ref, vb_ref = rest[3 * n_pages + 1:3 * n_pages + 5]
    past = n_pages * page
    new_rows = ikn_ref.shape[0]
    nrows = past + new_rows
    hd = kn_ref.shape[1]
    n_kv = kn_ref.shape[0] // new_rows

    blocks = [(ik_pages[p], k_pages[p], v_pages[p], p * page, page) for p in range(n_pages)]
    blocks.append((ikn_ref, kn_ref, vn_ref, past, new_rows))
    for ik_blk, k_blk, v_blk, r0, rows in blocks:
        ikb_ref[r0:r0 + rows, :] = ik_blk[...].astype(BF16)
        for kvh in range(n_kv):
            kb_ref[r0:r0 + rows, kvh * hd:(kvh + 1) * hd] = k_blk[pl.ds(kvh, rows, stride=n_kv), :].astype(BF16)
            vb_ref[r0:r0 + rows, kvh * hd:(kvh + 1) * hd] = v_blk[pl.ds(kvh, rows, stride=n_kv), :].astype(BF16)

    lane_in = lax.broadcasted_iota(I32, (LANES, LANES), 0) % t_new
    lane_out = lax.broadcasted_iota(I32, (LANES, LANES), 1) % t_new
    head_sum = jnp.where(lane_in == lane_out, 1.0, 0.0).astype(BF16)
    d = lax.dot_general(ikb_ref[...], iq_ref[...], NT_DIMS, preferred_element_type=F32)
    part = jnp.maximum(d, 0.0) * iw_ref[...]
    score = jnp.zeros((nrows, LANES), F32)
    for _ in range(3):
        piece = part.astype(BF16)
        score = score + jnp.dot(piece, head_sum, preferred_element_type=F32)
        part = part - piece.astype(F32)
    new_r = lax.broadcasted_iota(I32, (nrows, LANES), 0) - past
    lane_q = lax.broadcasted_iota(I32, (nrows, LANES), 1) % t_new
    keys_ref[0:nrows, :] = _sortable_key(jnp.where(new_r <= lane_q, score, -jnp.inf))
    keys_ref[nrows:, :] = jnp.full((keys_ref.shape[0] - nrows, LANES), NEG_INF_KEY, I32)

    thr = _topk_threshold(keys_ref, 1, keys_ref.shape[0], k_sel, LANES, keys_ref.shape[0], period=t_new)

    s = lax.dot_general(kb_ref[...], q_ref[...], NT_DIMS, preferred_element_type=F32)
    s = jnp.where(keys_ref[0:nrows, :] >= thr, s * sm_scale, NEG_BIG)
    m = jnp.max(s, axis=0, keepdims=True)
    pr = jnp.exp(s - m).astype(BF16)
    lsum = lax.dot_general(pr, jnp.ones((nrows, LANES), BF16), TN_DIMS, preferred_element_type=F32)
    out = lax.dot_general(pr, vb_ref[...], TN_DIMS, preferred_element_type=F32)
    o_ref[...] = out / jnp.concatenate([lsum] * n_kv, axis=1)


def sample_attention(iq_rows, iw_row, q_blk, ik_new, k_new, v_new, cache_idx_k, cache_k, cache_v,
                     page_table, t_new, k_sel):
    bd, n_pages = page_table.shape
    page = cache_idx_k.shape[1]
    idx_dim = cache_idx_k.shape[2]
    n_kv, hd = cache_k.shape[2], cache_k.shape[3]
    kvw = n_kv * hd
    qw = q_blk.shape[1]
    assert qw == LANES and hd == LANES
    new_rows = ik_new.shape[1]
    nrows = n_pages * page + new_rows
    cache_k = cache_k.reshape(-1, page * n_kv, hd)
    cache_v = cache_v.reshape(-1, page * n_kv, hd)
    k_new = k_new.reshape(bd, new_rows * n_kv, hd)
    v_new = v_new.reshape(bd, new_rows * n_kv, hd)

    def page_spec(rows, p):
        return pl.BlockSpec((None, rows, hd), lambda b, pt: (pt[b * n_pages + p], 0, 0))

    def per_seq(shape):
        return pl.BlockSpec((None,) + shape, lambda b, pt: (b,) + (0,) * len(shape))

    in_specs = [per_seq((LANES, idx_dim)), per_seq((1, LANES)), per_seq((qw, kvw)),
                per_seq((new_rows, idx_dim)), per_seq((new_rows * n_kv, hd)), per_seq((new_rows * n_kv, hd))]
    in_specs += [page_spec(page, p) for p in range(n_pages)]
    in_specs += [page_spec(page * n_kv, p) for p in range(n_pages)]
    in_specs += [page_spec(page * n_kv, p) for p in range(n_pages)]
    kern = functools.partial(_attn_sample_kernel, n_pages=n_pages, page=page, t_new=t_new, k_sel=k_sel,
                             sm_scale=LANES ** -0.5)
    return pl.pallas_call(
        kern,
        grid_spec=pltpu.PrefetchScalarGridSpec(
            num_scalar_prefetch=1,
            grid=(bd,),
            in_specs=in_specs,
            out_specs=pl.BlockSpec((None, qw, kvw), lambda b, pt: (b, 0, 0)),
            scratch_shapes=[pltpu.VMEM((-(-nrows // 64) * 64, LANES), I32),
                            pltpu.VMEM((nrows, idx_dim), BF16),
                            pltpu.VMEM((nrows, kvw), BF16), pltpu.VMEM((nrows, kvw), BF16)]),
        out_shape=jax.ShapeDtypeStruct((bd, qw, kvw), F32),
        compiler_params=_cparams("arbitrary"),
        name="sample_attention",
    )(page_table.reshape(-1), iq_rows, iw_row, q_blk, ik_new, k_new, v_new,
      *([cache_idx_k] * n_pages), *([cache_k] * n_pages), *([cache_v] * n_pages))


def _pool_prompt_kernel(p_ref, halo_ref, w_ref, ps_ref, o_ref, ext_ref, *, tm):
    sblk = pl.program_id(1)
    c = p_ref.shape[-1]
    gc = c // len(POOL_WINDOWS)
    od = o_ref.shape[-1] // len(POOL_WINDOWS)
    ext_ref[0:POOL_HALO, :] = jnp.where(sblk > 0, halo_ref[...], 0.0)
    ext_ref[POOL_HALO:POOL_HALO + tm, :] = p_ref[...]
    pos = sblk * tm + lax.broadcasted_iota(I32, (tm, gc), 0)
    for g, w in enumerate(POOL_WINDOWS):
        cols = slice(g * gc, (g + 1) * gc)
        tot = ext_ref[POOL_HALO:POOL_HALO + tm, cols]
        for dlt in range(1, w):
            tot = tot + ext_ref[POOL_HALO - dlt:POOL_HALO - dlt + tm, cols]
        cnt = jnp.minimum(pos + 1, w).astype(F32)
        pooled = tot / cnt - p_ref[:, cols]
        out = jnp.dot(pooled.astype(BF16), w_ref[g], preferred_element_type=F32)
        o_ref[:, g * od:(g + 1) * od] = (out * ps_ref[:, g * od:(g + 1) * od]).astype(o_ref.dtype)


def pool_prompt(p_in, w_pool, pool_scale, tm=512):
    b, s, c = p_in.shape
    g, gc, od = w_pool.shape
    tm = min(tm, s)
    nsb = s // tm
    hb = tm // POOL_HALO
    return pl.pallas_call(
        functools.partial(_pool_prompt_kernel, tm=tm),
        grid=(b, nsb),
        in_specs=[pl.BlockSpec((None, tm, c), lambda bi, si: (bi, si, 0)),
                  pl.BlockSpec((None, POOL_HALO, c), lambda bi, si: (bi, jnp.maximum(si * hb - 1, 0), 0)),
                  pl.BlockSpec((g, gc, od), lambda bi, si: (0, 0, 0)),
                  pl.BlockSpec((1, g * od), lambda bi, si: (0, 0))],
        out_specs=pl.BlockSpec((tm, g * od), lambda bi, si: (bi * nsb + si, 0)),
        out_shape=jax.ShapeDtypeStruct((b * s, g * od), F32),
        scratch_shapes=[pltpu.VMEM((POOL_HALO + tm, c), F32)],
        compiler_params=_cparams("arbitrary", "arbitrary"),
        name="pool_prompt",
    )(p_in, p_in, w_pool, pool_scale.reshape(1, g * od))


def _pool_sample_kernel(pad_ref, w_ref, ps_ref, o_ref, *, t_new, past):
    n_state = pad_ref.shape[0] - t_new
    c = pad_ref.shape[-1]
    gc = c // len(POOL_WINDOWS)
    od = o_ref.shape[-1] // len(POOL_WINDOWS)
    for t in range(t_new):
        for g, w in enumerate(POOL_WINDOWS):
            cols = slice(g * gc, (g + 1) * gc)
            cur = pad_ref[n_state + t, :, cols]
            tot = cur
            for dlt in range(1, w):
                tot = tot + pad_ref[n_state + t - dlt, :, cols]
            pooled = tot / float(min(past + t + 1, w)) - cur
            out = jnp.dot(pooled.astype(BF16), w_ref[g], preferred_element_type=F32)
            o_ref[t, :, g * od:(g + 1) * od] = out * ps_ref[:, g * od:(g + 1) * od]


def pool_sample(pad_t, w_pool, pool_scale, t_new, past):
    _, bd, c = pad_t.shape
    g, gc, od = w_pool.shape
    return pl.pallas_call(
        functools.partial(_pool_sample_kernel, t_new=t_new, past=past),
        out_shape=jax.ShapeDtypeStruct((t_new, bd, g * od), F32),
        compiler_params=pltpu.CompilerParams(vmem_limit_bytes=VMEM_LIMIT),
        name="pool_sample",
    )(pad_t, w_pool, pool_scale.reshape(1, g * od))


def _merge_kernel(a_ref, w_ref, ga_ref, gb_ref, pool_ref, o_ref):
    acc = jnp.dot(a_ref[...], w_ref[...], preferred_element_type=F32)
    o_ref[...] = (ga_ref[...].astype(F32) * acc + gb_ref[...].astype(F32) * pool_ref[...]).astype(o_ref.dtype)


def merge_branches(attn, w_attn_proj, sig_a, sig_b, pool_out, tm=1024, tn=512):
    n, k = attn.shape
    d = w_attn_proj.shape[1]
    tm, tn = min(tm, n), min(tn, d)
    blk = pl.BlockSpec((tm, tn), lambda i, j: (i, j))
    return pl.pallas_call(
        _merge_kernel,
        grid=(n // tm, d // tn),
        in_specs=[pl.BlockSpec((tm, k), lambda i, j: (i, 0)),
                  pl.BlockSpec((k, tn), lambda i, j: (0, j)), blk, blk, blk],
        out_specs=blk,
        out_shape=jax.ShapeDtypeStruct((n, d), BF16),
        compiler_params=_cparams("arbitrary", "arbitrary"),
        name="merge_branches",
    )(attn, w_attn_proj, sig_a, sig_b, pool_out)


def _out_proj_kernel(m_ref, w_ref, x_ref, gate_ref, o_ref):
    acc = jnp.dot(m_ref[...], w_ref[...], preferred_element_type=F32)
    o_ref[...] = x_ref[...] + gate_ref[...] * acc


def out_project(merged, w_out, x, gate, rows_per_group, tm=1024, tn=512):
    n, k = merged.shape
    d = w_out.shape[1]
    tm, tn = min(tm, n), min(tn, d)
    if gate.ndim == 3:
        tm = min(tm, rows_per_group)
        bpg = rows_per_group // tm
        gate_spec = pl.BlockSpec((None, 1, tn), lambda i, j: (i // bpg, 0, j))
    else:
        gate_spec = pl.BlockSpec((tm, tn), lambda i, j: (i, j))
    blk = pl.BlockSpec((tm, tn), lambda i, j: (i, j))
    return pl.pallas_call(
        _out_proj_kernel,
        grid=(n // tm, d // tn),
        in_specs=[pl.BlockSpec((tm, k), lambda i, j: (i, 0)),
                  pl.BlockSpec((k, tn), lambda i, j: (0, j)), blk, gate_spec],
        out_specs=blk,
        out_shape=jax.ShapeDtypeStruct((n, d), F32),
        compiler_params=_cparams("arbitrary", "arbitrary"),
        name="out_project",
    )(merged, w_out, x, gate)


def _extract_top(work, order, n_top, on_hit, exact):
    vals = []
    for r in range(n_top):
        m = jnp.max(work, axis=0, keepdims=True)
        hit = work == m
        if exact:
            first = jnp.min(jnp.where(hit, order, float(2 ** 20)), axis=0, keepdims=True)
            hit = order == first
        work = jnp.where(hit, -jnp.inf, work)
        on_hit(r, hit)
        vals.append(m)
    return vals


def _top_lists(s, n_top, exact):
    rows, t = s.shape
    order = lax.broadcasted_iota(I32, (rows, t), 0).astype(F32)
    state = {"rank": jnp.full((rows, t), float(rows), F32)}

    def on_hit(r, hit):
        state["rank"] = jnp.where(hit, float(r), state["rank"])

    vals = _extract_top(s, order, n_top, on_hit, exact)
    return jnp.concatenate(vals, axis=0), state["rank"]


def _peer_route_kernel(pq_ref, keys_ref, r2_ref, e2_ref, n1_ref, e1_ref, *, n_heads, n_top):
    t = pq_ref.shape[1]
    pieces = []
    for a in range(n_top):
        nb = n_top // (a + 1)
        rows = -(-nb // 8) * 8
        pieces.append((a, nb, rows))
    flat_idx = jnp.concatenate(
        [a * n_top + lax.broadcasted_iota(I32, (rows, t), 0) for a, _, rows in pieces], axis=0).astype(F32)
    n_cand = flat_idx.shape[0]

    def route(h, exact):
        s1 = lax.dot_general(keys_ref[2 * h], pq_ref[2 * h], NT_DIMS, preferred_element_type=F32)
        s2 = lax.dot_general(keys_ref[2 * h + 1], pq_ref[2 * h + 1], NT_DIMS, preferred_element_type=F32)
        v1, rank1 = _top_lists(s1, n_top, exact)
        v2, rank2 = _top_lists(s2, n_top, exact)
        cand = jnp.concatenate(
            [jnp.where(lax.broadcasted_iota(I32, (rows, t), 0) < nb, v1[a:a + 1] + v2[:rows], -jnp.inf)
             for a, nb, rows in pieces], axis=0)
        state = {"chosen": jnp.zeros((n_cand, t), F32)}

        def on_hit(_, hit):
            state["chosen"] = jnp.where(hit, 1.0, state["chosen"])

        top_vals = _extract_top(cand, flat_idx, n_top, on_hit, exact)
        chosen = state["chosen"]
        z = sum(jnp.exp(m - top_vals[0]) for m in top_vals)
        n1 = jnp.zeros_like(rank1)
        r0 = 0
        for a, _, rows in pieces:
            n_a = jnp.sum(chosen[r0:r0 + rows], axis=0, keepdims=True)
            n1 = jnp.where(rank1 == float(a), n_a, n1)
            r0 += rows
        inv_z = 1.0 / z
        r2_ref[h] = rank2.astype(r2_ref.dtype)
        e2_ref[h] = jnp.where(rank2 < float(n_top), jnp.exp(s2 - v2[0:1]), 0.0).astype(e2_ref.dtype)
        n1_ref[h] = n1
        e1_ref[h] = jnp.where(rank1 < float(n_top), jnp.exp(s1 - v1[0:1]), 0.0) * inv_z
        taken = [jnp.sum(jnp.where(rank < float(n_top), 1.0, 0.0), axis=0, keepdims=True) for rank in (rank1, rank2)]
        taken.append(jnp.sum(chosen, axis=0, keepdims=True))
        return sum(jnp.abs(c - float(n_top)) for c in taken)

    def route_head(h, carry):
        excess = route(h, exact=False)

        @pl.when(jnp.max(excess) > 0.0)
        def _():
            route(h, exact=True)

        return carry

    lax.fori_loop(0, n_heads, route_head, 0)


def peer_route(pq_hm, keys_flat, n_heads, tp=256):
    n = pq_hm.shape[1]
    nk, half = keys_flat.shape[1], keys_flat.shape[2]
    tp = min(tp, n)
    out_spec = pl.BlockSpec((n_heads, nk, tp), lambda i: (0, 0, i))
    return pl.pallas_call(
        functools.partial(_peer_route_kernel, n_heads=n_heads, n_top=PEER_TOPK),
        grid=(n // tp,),
        in_specs=[pl.BlockSpec((2 * n_heads, tp, half), lambda i: (0, i, 0)),
                  pl.BlockSpec(keys_flat.shape, lambda i: (0, 0, 0))],
        out_specs=[out_spec] * 4,
        out_shape=[jax.ShapeDtypeStruct((n_heads, nk, n), dt) for dt in (BF16, BF16, F32, F32)],
        compiler_params=_cparams("arbitrary"),
        name="peer_route",
    )(pq_hm, keys_flat)


def _peer_expert_kernel(ht_ref, u_ref, vt_ref, r2_ref, e2_ref, n1_ref, e1_ref, x_ref, gate_ref, o_ref,
                        a_ref, w_ref, acc_ref, n1s_ref, e1s_ref, *, n_heads, nk):
    e = pl.program_id(1)
    eb, tm = a_ref.shape
    n_sub = eb // nk

    @pl.when(e == 0)
    def _():
        acc_ref[...] = jnp.zeros_like(acc_ref)

    a_ref[...] = jnp.dot(u_ref[...], ht_ref[...], preferred_element_type=F32)

    for h in range(n_heads):
        for s in range(n_sub):
            n1s_ref[h, s] = jnp.broadcast_to(n1_ref[h, s:s + 1, :], (ROW_BLOCK, tm)).astype(BF16)
            e1s_ref[h, s] = jnp.broadcast_to(e1_ref[h, s:s + 1, :], (ROW_BLOCK, tm)).astype(BF16)

    def sub_block(s, carry):
        r0 = pl.multiple_of(s * nk, nk)
        for rb in range(nk // ROW_BLOCK):
            rr = slice(rb * ROW_BLOCK, (rb + 1) * ROW_BLOCK)
            gsum = jnp.zeros((ROW_BLOCK, tm), BF16)
            for h in range(n_heads):
                keep = r2_ref[h, rr, :] < n1s_ref[h, s]
                gsum = gsum + jnp.where(keep, e2_ref[h, rr, :], jnp.zeros((), BF16)) * e1s_ref[h, s]
            a = a_ref[pl.ds(r0 + rb * ROW_BLOCK, ROW_BLOCK), :]
            act = 0.5 * a * (1.0 + lax.erf(a * (2.0 ** -0.5)))
            w_ref[pl.ds(r0 + rb * ROW_BLOCK, ROW_BLOCK), :] = gsum * act.astype(BF16)
        return carry

    lax.fori_loop(0, n_sub, sub_block, 0)
    acc_ref[...] += jnp.dot(vt_ref[...], w_ref[...], preferred_element_type=F32)

    @pl.when(e == pl.num_programs(1) - 1)
    def _():
        o_ref[...] = x_ref[...] + gate_ref[...] * acc_ref[...].T


def peer_experts(h2_t, u, v_t, r2, e2, n1, e1, x, gate, rows_per_group, tm=512, eb=1024):
    d, n = h2_t.shape
    n_exp = u.shape[0]
    n_heads, nk = r2.shape[0], r2.shape[1]
    tm, eb = min(tm, n), min(eb, n_exp)
    assert tm % LANES == 0 and eb % nk == 0 and (eb // nk) % 8 == 0 and nk % ROW_BLOCK == 0
    n_sub = eb // nk
    tok_tile = pl.BlockSpec((n_heads, nk, tm), lambda i, e: (0, 0, i))
    i1_tile = pl.BlockSpec((n_heads, n_sub, tm), lambda i, e: (0, e, i))
    return pl.pallas_call(
        functools.partial(_peer_expert_kernel, n_heads=n_heads, nk=nk),
        grid=(n // tm, n_exp // eb),
        in_specs=[pl.BlockSpec((d, tm), lambda i, e: (0, i)),
                  pl.BlockSpec((eb, d), lambda i, e: (e, 0)),
                  pl.BlockSpec((d, eb), lambda i, e: (0, e)),
                  tok_tile, tok_tile, i1_tile, i1_tile,
                  pl.BlockSpec((tm, d), lambda i, e: (i, 0)),
                  _mod_spec(gate, tm, rows_per_group)],
        out_specs=pl.BlockSpec((tm, d), lambda i, e: (i, 0)),
        out_shape=jax.ShapeDtypeStruct((n, d), F32),
        scratch_shapes=[pltpu.VMEM((eb, tm), F32), pltpu.VMEM((eb, tm), BF16), pltpu.VMEM((d, tm), F32),
                        pltpu.VMEM((n_heads, n_sub, ROW_BLOCK, tm), BF16),
                        pltpu.VMEM((n_heads, n_sub, ROW_BLOCK, tm), BF16)],
        compiler_params=_cparams("arbitrary", "arbitrary"),
        name="peer_experts",
    )(h2_t, u, v_t, r2, e2, n1, e1, x, gate)


def _rope_tables(pos, half):
    freqs = ROPE_THETA ** (-jnp.arange(half, dtype=F32) / half)
    ang = pos.astype(F32)[:, None] * freqs[None, :]
    cos, sin = jnp.cos(ang), jnp.sin(ang)
    return jnp.concatenate([cos, cos], axis=1), jnp.concatenate([-sin, sin], axis=1)


def _channel_mixer(x_mid, shift2, scale2, gate2, rows_per_group, norm2_g, w_peer_q, keys_flat, peer_u, peer_vt,
                   n_peer_heads, tm):
    h2, h2_t = norm_modulate(x_mid, norm2_g, scale2, shift2, rows_per_group, tm, with_transpose=True)
    (pq_hm,) = project(h2, w_peer_q, 0, w_peer_q.shape[1], outs=(("head_major", BF16),))
    r2, e2, n1, e1 = peer_route(pq_hm, keys_flat, n_peer_heads)
    return peer_experts(h2_t, peer_u, peer_vt, r2, e2, n1, e1, x_mid, gate2, rows_per_group, tm=tm)


def kernel(x_prompt, x_sample, cache_k, cache_v, cache_idx_k, state_pool, page_table, c_prompt, c_sample,
           norm1_g, norm2_g, w_mod, b_mod, w_in, q_norm_g, k_norm_g, w_attn_proj, w_pool, pool_scale,
           w_out, w_peer_q, peer_keys, peer_u, peer_v):
    b, s, d = x_prompt.shape
    bd, t_new, _ = x_sample.shape
    hd = q_norm_g.shape[0]
    idx_dim = cache_idx_k.shape[2]
    page = cache_k.shape[1]
    n_kv = cache_k.shape[2]
    n_heads = w_attn_proj.shape[0] // hd
    d_pool = state_pool.shape[2]
    n_state = state_pool.shape[1]
    q_cols, kv_cols = n_heads * hd, n_kv * hd
    idx_heads = (w_in.shape[1] - q_cols - 2 * kv_cols - idx_dim - d_pool - 2 * d) // (idx_dim + 1)
    iq_cols = idx_heads * idx_dim
    past = page_table.shape[1] * page
    n_peer_heads, _, nk, half = peer_keys.shape
    assert hd == LANES and idx_dim == LANES and nk == LANES and half == LANES

    o_ik = q_cols + 2 * kv_cols + iq_cols
    o_iw = o_ik + idx_dim
    iw_pad = jnp.zeros((d, LANES - idx_heads), BF16)
    w_in_c = w_in.astype(BF16)
    w_in_b = jnp.concatenate([w_in_c[:, :o_ik], w_in_c[:, o_iw + idx_heads:], w_in_c[:, o_ik:o_iw],
                              w_in_c[:, o_iw:o_iw + idx_heads], iw_pad], axis=1)
    c_q, c_k, c_v = 0, q_cols, q_cols + kv_cols
    c_iq = c_v + kv_cols
    c_pool = c_iq + iq_cols
    c_ga = c_pool + d_pool
    c_gb = c_ga + d
    c_ik = c_gb + d
    c_iw = c_ik + idx_dim
    w_ap_b = w_attn_proj.astype(BF16)
    w_out_b = w_out.astype(BF16)
    w_pq_b = w_peer_q.astype(BF16)
    w_pool_b = w_pool.astype(BF16)
    keys_flat = peer_keys.reshape(n_peer_heads * 2, nk, half).astype(BF16)
    peer_u_b = peer_u.astype(BF16)
    peer_vt_b = peer_v.astype(BF16).T
    iw_scale = idx_heads ** -0.5 * idx_dim ** -0.5

    mod = modulation(jnp.concatenate([c_prompt, c_sample], axis=0), w_mod, b_mod)
    mod = mod.reshape(b + bd, 6, d)
    mod_p = [mod[:b, i][:, None, :] for i in range(6)]
    mod_s = [jnp.repeat(mod[b:, i], t_new, axis=0) for i in range(6)]

    def mixer_inputs(x2, shift, scale, pos, rows_per_group, tm):
        cos, sin = (jnp.tile(tbl, (x2.shape[0] // pos.shape[0], 1)) for tbl in _rope_tables(pos, hd // 2))
        (h,) = norm_modulate(x2, norm1_g, scale, shift, rows_per_group, tm)
        (q_hm,) = project(h, w_in_b, c_q, q_cols, outs=(("head_major", BF16),), norm_g=q_norm_g, cos=cos, sin=sin, tm=tm)
        k_f, k_b = project(h, w_in_b, c_k, kv_cols, outs=(("token_major", F32), ("token_major", BF16)),
                           norm_g=k_norm_g, cos=cos, sin=sin, tm=tm)
        v_f, v_b = project(h, w_in_b, c_v, kv_cols, outs=(("token_major", F32), ("token_major", BF16)), tm=tm)
        (iq_hm,) = project(h, w_in_b, c_iq, iq_cols, outs=(("head_major", BF16),), cos=cos, sin=sin, tm=tm)
        ik_f, ik_b = project(h, w_in_b, c_ik, idx_dim, outs=(("token_major", F32), ("token_major", BF16)),
                             cos=cos, sin=sin, tm=tm)
        (iw,) = project(h, w_in_b, c_iw, LANES, outs=(("token_major", F32),), scale=iw_scale, tm=tm)
        (p_in,) = project(h, w_in_b, c_pool, d_pool, outs=(("token_major", F32),), tm=tm)
        (sig_a,) = project(h, w_in_b, c_ga, d, outs=(("token_major", BF16),), act="sigmoid", tm=tm)
        (sig_b,) = project(h, w_in_b, c_gb, d, outs=(("token_major", BF16),), act="sigmoid", tm=tm)
        return q_hm, k_f, k_b, v_f, v_b, iq_hm, ik_f, ik_b, iw[:, :idx_heads], p_in, sig_a, sig_b

    xp = x_prompt.reshape(b * s, d)
    tm_p = min(1024, s)
    pos_p = jnp.arange(s)
    (q_hm, k_f, k_b, v_f, v_b, iq_hm, ik_f, ik_b, iw, p_in, sig_a, sig_b) = mixer_inputs(
        xp, mod_p[0], mod_p[1], pos_p, s, tm_p)
    attn_p = prompt_attention(iq_hm, iw.T, ik_b, q_hm, k_b, v_b, b, s, min(TOPK_MAX, s // 4))
    pool_p = pool_prompt(p_in.reshape(b, s, d_pool), w_pool_b, pool_scale)
    merged = merge_branches(attn_p, w_ap_b, sig_a, sig_b, pool_p)
    h_p = out_project(merged, w_out_b, xp, mod_p[2], s)
    y_prompt = _channel_mixer(h_p, mod_p[3], mod_p[4], mod_p[5], s, norm2_g, w_pq_b, keys_flat, peer_u_b,
                              peer_vt_b, n_peer_heads, min(512, s))
    k_prompt = k_f.reshape(b, s, n_kv, hd)
    v_prompt = v_f.reshape(b, s, n_kv, hd)
    idx_k_prompt = ik_f.reshape(b, s, idx_dim)
    pool_state_prompt = p_in.reshape(b, s, d_pool)[:, s - n_state:, :]

    ns = bd * t_new
    xs = x_sample.reshape(ns, d)
    pos_s = past + jnp.arange(t_new)
    (q_hm, k_f, k_b, v_f, v_b, iq_hm, ik_f, ik_b, iw, p_in, sig_a, sig_b) = mixer_inputs(
        xs, mod_s[0], mod_s[1], pos_s, t_new, ns)
    group = n_heads // n_kv
    iq_rows = iq_hm.reshape(idx_heads, bd, t_new, idx_dim).transpose(1, 0, 2, 3).reshape(bd, idx_heads * t_new, idx_dim)
    iq_rows = jnp.pad(iq_rows, ((0, 0), (0, LANES - idx_heads * t_new), (0, 0)))
    iw_row = iw.reshape(bd, t_new, idx_heads).transpose(0, 2, 1).reshape(bd, 1, idx_heads * t_new)
    iw_row = jnp.pad(iw_row, ((0, 0), (0, 0), (0, LANES - idx_heads * t_new)))
    q5 = q_hm.reshape(n_kv, group, bd, t_new, hd).transpose(2, 0, 1, 3, 4)
    eye = jnp.eye(n_kv, dtype=q5.dtype)
    q_blk = (q5[:, :, :, :, None, :] * eye[None, :, None, None, :, None]).reshape(
        bd, n_kv * group * t_new, n_kv * hd)
    q_blk = jnp.pad(q_blk, ((0, 0), (0, LANES - n_kv * group * t_new), (0, 0)))
    new_rows = 16
    def pad_new(a, *tail):
        a = a.reshape((bd, t_new) + tail)
        return jnp.pad(a, ((0, 0), (0, new_rows - t_new)) + ((0, 0),) * len(tail))

    att = sample_attention(iq_rows, iw_row, q_blk, pad_new(ik_f, idx_dim), pad_new(k_f, n_kv, hd),
                           pad_new(v_f, n_kv, hd), cache_idx_k, cache_k, cache_v,
                           page_table, t_new, min(TOPK_MAX, (past + t_new) // 4))
    att = att[:, :n_kv * group * t_new].reshape(bd, n_kv, group, t_new, n_kv, hd)
    att = jnp.einsum('bkgqkd->bqkgd', att).reshape(ns, q_cols).astype(BF16)
    pad_t = jnp.concatenate([state_pool, p_in.reshape(bd, t_new, d_pool)], axis=1)
    pool_s = pool_sample(pad_t.transpose(1, 0, 2), w_pool_b, pool_scale, t_new, past)
    pool_s = pool_s.transpose(1, 0, 2).reshape(ns, d)
    merged = merge_branches(att, w_ap_b, sig_a, sig_b, pool_s)
    h_s = out_project(merged, w_out_b, xs, mod_s[2], t_new)
    y_sample = _channel_mixer(h_s, mod_s[3], mod_s[4], mod_s[5], t_new, norm2_g, w_pq_b, keys_flat, peer_u_b,
                              peer_vt_b, n_peer_heads, ns)

    return (y_prompt.reshape(b, s, d), y_sample.reshape(bd, t_new, d), k_prompt, v_prompt, idx_k_prompt,
            pool_state_prompt, k_f.reshape(bd, t_new, n_kv, hd), v_f.reshape(bd, t_new, n_kv, hd),
            ik_f.reshape(bd, t_new, idx_dim), pad_t[:, t_new:, :])
```

```python
import functools
import math

import jax
import jax.numpy as jnp
from jax import lax
from jax.experimental import pallas as pl
from jax.experimental.pallas import tpu as pltpu

F32 = jnp.float32
BF16 = jnp.bfloat16
I32 = jnp.int32

LANES = 128
EPS = 1e-6
ROPE_THETA = 10000.0
TOPK_MAX = 256
POOL_WINDOWS = (2, 4, 8, 16)
POOL_HALO = 16
PEER_TOPK = 16
ROW_BLOCK = 16
VMEM_LIMIT = 56 * 1024 * 1024

NT_DIMS = (((1,), (1,)), ((), ()))
TN_DIMS = (((0,), (0,)), ((), ()))
NEG_BIG = -1e30
INT_MIN = -2 ** 31


def _cparams(*sem):
    return pltpu.CompilerParams(dimension_semantics=sem, vmem_limit_bytes=VMEM_LIMIT)


def _mod_kernel(c_ref, w_ref, b_ref, o_ref):
    c = c_ref[...]
    a = c * jax.nn.sigmoid(c)
    o_ref[...] = jnp.dot(a, w_ref[...], preferred_element_type=F32,
                         precision=lax.Precision.HIGHEST) + b_ref[...]


def modulation(c, w_mod, b_mod, tn=1024):
    n, d = c.shape
    cols = w_mod.shape[1]
    return pl.pallas_call(
        _mod_kernel,
        grid=(cols // tn,),
        in_specs=[pl.BlockSpec((n, d), lambda j: (0, 0)),
                  pl.BlockSpec((d, tn), lambda j: (0, j)),
                  pl.BlockSpec((1, tn), lambda j: (0, j))],
        out_specs=pl.BlockSpec((n, tn), lambda j: (0, j)),
        out_shape=jax.ShapeDtypeStruct((n, cols), F32),
        compiler_params=_cparams("arbitrary"),
        name="modulation",
    )(c, w_mod, b_mod.reshape(1, cols))


def _norm_mod_kernel(x_ref, g_ref, sc_ref, sh_ref, o_ref, *maybe_ot_ref):
    x = x_ref[...]
    y = x * lax.rsqrt(jnp.mean(x * x, axis=-1, keepdims=True) + EPS) * g_ref[...]
    h = y * (1.0 + sc_ref[...]) + sh_ref[...]
    o_ref[...] = h.astype(o_ref.dtype)
    for ot_ref in maybe_ot_ref:
        ot_ref[...] = h.T.astype(ot_ref.dtype)


def _mod_spec(arr, tm, rows_per_group):
    if arr.ndim == 3:
        bpg = rows_per_group // tm
        return pl.BlockSpec((None, 1, arr.shape[-1]), lambda i, *_: (i // bpg, 0, 0))
    return pl.BlockSpec((tm, arr.shape[-1]), lambda i, *_: (i, 0))


def norm_modulate(x, g, scale, shift, rows_per_group, tm, with_transpose=False):
    n, d = x.shape
    out_specs = [pl.BlockSpec((tm, d), lambda i: (i, 0))]
    out_shape = [jax.ShapeDtypeStruct((n, d), BF16)]
    if with_transpose:
        out_specs.append(pl.BlockSpec((d, tm), lambda i: (0, i)))
        out_shape.append(jax.ShapeDtypeStruct((d, n), BF16))
    return pl.pallas_call(
        _norm_mod_kernel,
        grid=(n // tm,),
        in_specs=[pl.BlockSpec((tm, d), lambda i: (i, 0)),
                  pl.BlockSpec((1, d), lambda i: (0, 0)),
                  _mod_spec(scale, tm, rows_per_group),
                  _mod_spec(shift, tm, rows_per_group)],
        out_specs=out_specs,
        out_shape=out_shape,
        compiler_params=_cparams("arbitrary"),
        name="norm_modulate",
    )(x, g.reshape(1, d), scale, shift)


def _proj_kernel(*refs, norm, rope, act, scale, outs):
    h_ref, w_ref = refs[0], refs[1]
    pos = 2
    g_ref = cos_ref = sin_ref = None
    if norm:
        g_ref = refs[pos]
        pos += 1
    if rope:
        cos_ref, sin_ref = refs[pos], refs[pos + 1]
        pos += 2
    out_refs = refs[pos:]
    acc = jnp.dot(h_ref[...], w_ref[...], preferred_element_type=F32)
    tn = acc.shape[1]
    if norm or rope:
        slabs = []
        for u in range(tn // LANES):
            t = acc[:, u * LANES:(u + 1) * LANES]
            if norm:
                msq = jnp.dot((t * t).astype(BF16), jnp.full((LANES, LANES), 1.0 / LANES, BF16),
                              preferred_element_type=F32)
                t = t * lax.rsqrt(msq + EPS) * g_ref[...]
            if rope:
                t = t * cos_ref[...] + pltpu.roll(t, LANES // 2, 1) * sin_ref[...]
            slabs.append(t)
    else:
        if scale != 1.0:
            acc = acc * scale
        if act == "sigmoid":
            acc = jax.nn.sigmoid(acc)
        slabs = None
    for kind, o_ref in zip(outs, out_refs):
        if kind == "head_major":
            for u in range(tn // LANES):
                t = acc[:, u * LANES:(u + 1) * LANES] if slabs is None else slabs[u]
                o_ref[u] = t.astype(o_ref.dtype)
        elif slabs is not None:
            for u, t in enumerate(slabs):
                o_ref[:, u * LANES:(u + 1) * LANES] = t.astype(o_ref.dtype)
        else:
            o_ref[...] = acc.astype(o_ref.dtype)


def project(h, w, col0, ncols, *, outs, norm_g=None, cos=None, sin=None, act="none", scale=1.0,
            tm=1024, tn=512):
    n, k = h.shape
    tn = min(tn, ncols)
    tm = min(tm, n)
    assert col0 % tn == 0 and ncols % tn == 0 and n % tm == 0
    c0 = col0 // tn
    in_specs = [pl.BlockSpec((tm, k), lambda i, j: (i, 0)),
                pl.BlockSpec((k, tn), lambda i, j: (0, j + c0))]
    args = [h, w]
    if norm_g is not None:
        in_specs.append(pl.BlockSpec((1, LANES), lambda i, j: (0, 0)))
        args.append(norm_g.reshape(1, LANES))
    if cos is not None:
        in_specs += [pl.BlockSpec((tm, LANES), lambda i, j: (i, 0))] * 2
        args += [cos, sin]
    out_specs, out_shapes = [], []
    for kind, dt in outs:
        if kind == "head_major":
            out_specs.append(pl.BlockSpec((tn // LANES, tm, LANES), lambda i, j: (j, i, 0)))
            out_shapes.append(jax.ShapeDtypeStruct((ncols // LANES, n, LANES), dt))
        else:
            out_specs.append(pl.BlockSpec((tm, tn), lambda i, j: (i, j)))
            out_shapes.append(jax.ShapeDtypeStruct((n, ncols), dt))
    res = pl.pallas_call(
        functools.partial(_proj_kernel, norm=norm_g is not None, rope=cos is not None, act=act,
                          scale=scale, outs=tuple(kd for kd, _ in outs)),
        grid=(n // tm, ncols // tn),
        in_specs=in_specs,
        out_specs=out_specs,
        out_shape=out_shapes,
        compiler_params=_cparams("arbitrary", "arbitrary"),
        name="project",
    )(*args)
    return res


def _sortable_key(score):
    bits = pltpu.bitcast(score, I32)
    return jnp.where(bits < 0, bits ^ jnp.int32(0x7FFFFFFF), bits)


NEG_INF_KEY = (-8388608) ^ 0x7FFFFFFF


def _lane_group_sum(x, period):
    shift = period
    while shift < LANES:
        x = x + pltpu.roll(x, shift, 1)
        shift *= 2
    return x


def _topk_threshold(keys_ref, nchunks, ck, k, width, max_rows, period=None):
    acc_rows = math.gcd(ck, 64)

    def count(pred):
        def body(c, cnt):
            r0 = pl.multiple_of(c * ck, 8)
            kk = keys_ref[pl.ds(r0, ck), :]
            rows = r0 + lax.broadcasted_iota(I32, kk.shape, 0)
            hit = jnp.where(pred(kk, rows), 1.0, 0.0)
            return cnt + jnp.sum(hit.reshape(ck // acc_rows, acc_rows, width), axis=0)
        part = lax.fori_loop(0, nchunks, body, jnp.zeros((acc_rows, width), F32))
        return jnp.sum(part, axis=0, keepdims=True)

    if period is None:
        def bit_step(i, ut):
            uc = ut | jnp.left_shift(jnp.int32(1), jnp.int32(31) - i)
            cand = uc ^ jnp.int32(INT_MIN)
            return jnp.where(count(lambda kk, _: kk >= cand) >= float(k), uc, ut)
        ut = lax.fori_loop(0, 32, bit_step, jnp.zeros((1, width), I32))
    else:
        replica = lax.broadcasted_iota(I32, (1, width), 1) // period
        bits_per_pass = (width // period).bit_length() - 1
        same_col = jnp.where(lax.broadcasted_iota(I32, (width, width), 0) % period
                             == lax.broadcasted_iota(I32, (width, width), 1) % period, 1.0, 0.0).astype(BF16)
        ut = jnp.zeros((1, width), I32)
        hi = 32
        while hi > 0:
            nb = min(bits_per_pass, hi)
            lo = hi - nb
            cand = (ut | jnp.left_shift(replica, lo)) ^ jnp.int32(INT_MIN)
            ok = jnp.where(count(lambda kk, _, cand=cand: kk >= cand) >= float(k), 1.0, 0.0)
            ok = jnp.where(replica < 2 ** nb, ok, 0.0)
            n_ok = jnp.dot(jnp.broadcast_to(ok, (8, width)).astype(BF16), same_col,
                           preferred_element_type=F32)[0:1]
            best = n_ok.astype(I32) - 1
            ut = ut | jnp.left_shift(best, lo)
            hi = lo
    thr = ut ^ jnp.int32(INT_MIN)

    real = thr > jnp.int32(NEG_INF_KEY)
    surplus = jnp.where(real, count(lambda kk, _: kk >= thr) - float(k), 0.0)

    @pl.when(jnp.max(surplus) > 0.0)
    def _():
        need = float(k) - count(lambda kk, _: kk > thr)
        n_bits = max(1, (max_rows - 1).bit_length())

        def idx_step(i, jv):
            cand = jv | jnp.left_shift(jnp.int32(1), jnp.int32(n_bits - 1) - i)
            below = count(lambda kk, rows: jnp.where(kk == thr, rows, jnp.int32(2 ** 30)) < cand)
            return jnp.where(below < need, cand, jv)
        last = lax.fori_loop(0, n_bits, idx_step, jnp.zeros((1, width), I32))
        last = jnp.where(surplus > 0.0, last, jnp.int32(2 ** 30))

        def demote(c, carry):
            r0 = pl.multiple_of(c * ck, 8)
            kk = keys_ref[pl.ds(r0, ck), :]
            rows = r0 + lax.broadcasted_iota(I32, kk.shape, 0)
            keys_ref[pl.ds(r0, ck), :] = jnp.where(jnp.where(kk == thr, rows, 0) > last, thr - 1, kk)
            return carry
        lax.fori_loop(0, nchunks, demote, 0)

    return jnp.maximum(thr, jnp.int32(NEG_INF_KEY + 1))


def _attn_prompt_kernel(iq_ref, iw_ref, ik_ref, q_ref, k_ref, v_ref, o_ref, keys_ref, m_ref, l_ref, acc_ref, *,
                        nq, ck, k_sel, n_idx_heads, n_kv, group, sm_scale):
    j = pl.program_id(1)
    nchunks = (j * nq) // ck + 1
    hd = LANES

    qpos = j * nq + lax.broadcasted_iota(I32, (ck, nq), 1)

    def score_chunk(c, carry):
        r0 = pl.multiple_of(c * ck, ck)
        ikc = ik_ref[pl.ds(r0, ck), :]
        acc = jnp.zeros((ck, nq), F32)
        for hp in range(n_idx_heads // 2):
            rhs = iq_ref[2 * hp:2 * hp + 2].reshape(2 * nq, hd)
            d = lax.dot_general(ikc, rhs, NT_DIMS, preferred_element_type=F32)
            for u in range(2):
                h = 2 * hp + u
                acc = acc + jnp.maximum(d[:, u * nq:(u + 1) * nq], 0.0) * iw_ref[h:h + 1, :]
        kpos = r0 + lax.broadcasted_iota(I32, (ck, nq), 0)
        score = jnp.where(kpos <= qpos, acc, -jnp.inf)
        keys_ref[pl.ds(r0, ck), :] = _sortable_key(score)
        return carry

    lax.fori_loop(0, nchunks, score_chunk, 0)

    thr = _topk_threshold(keys_ref, nchunks, ck, k_sel, nq, keys_ref.shape[0])

    gw = group * nq
    exp2_scale = sm_scale * math.log2(math.e)
    m_ref[...] = jnp.full(m_ref.shape, NEG_BIG, F32)
    l_ref[...] = jnp.zeros(l_ref.shape, F32)
    acc_ref[...] = jnp.zeros(acc_ref.shape, F32)

    def attend(c, carry):
        r0 = pl.multiple_of(c * ck, ck)
        sel = keys_ref[pl.ds(r0, ck), :] >= thr
        for kvh in range(n_kv):
            qg = q_ref[group * kvh:group * (kvh + 1)].reshape(gw, hd)
            kc = k_ref[pl.ds(r0, ck), kvh * hd:(kvh + 1) * hd]
            s = lax.dot_general(kc, qg, NT_DIMS, preferred_element_type=F32)
            s = jnp.concatenate([jnp.where(sel, s[:, g * nq:(g + 1) * nq], NEG_BIG) for g in range(group)], axis=1)
            m = m_ref[kvh]
            m_new = jnp.maximum(m, jnp.max(s, axis=0, keepdims=True))
            alpha = jnp.exp2((m - m_new) * exp2_scale)
            p = jnp.exp2((s - m_new) * exp2_scale)
            m_ref[kvh] = m_new
            l_ref[kvh] = alpha * l_ref[kvh] + jnp.sum(p, axis=0, keepdims=True)
            vc = v_ref[pl.ds(r0, ck), kvh * hd:(kvh + 1) * hd]
            pv = lax.dot_general(vc, p.astype(BF16), TN_DIMS, preferred_element_type=F32)
            acc_ref[kvh] = alpha * acc_ref[kvh] + pv
        return carry

    lax.fori_loop(0, nchunks, attend, 0)
    for kvh in range(n_kv):
        out_t = acc_ref[kvh] / l_ref[kvh]
        for g in range(group):
            head = group * kvh + g
            o_ref[:, head * hd:(head + 1) * hd] = out_t[:, g * nq:(g + 1) * nq].T.astype(o_ref.dtype)


def prompt_attention(iq_hm, iw_t, ik, q_hm, k, v, batch, seq, k_sel, nq=128, ck=512):
    n_idx_heads = iq_hm.shape[0]
    n_heads = q_hm.shape[0]
    hd = LANES
    n_kv = k.shape[1] // hd
    group = n_heads // n_kv
    ck = min(ck, seq)
    nqb = seq // nq
    kern = functools.partial(_attn_prompt_kernel, nq=nq, ck=ck, k_sel=k_sel, n_idx_heads=n_idx_heads,
                             n_kv=n_kv, group=group, sm_scale=hd ** -0.5)
    return pl.pallas_call(
        kern,
        grid=(batch, nqb),
        in_specs=[pl.BlockSpec((n_idx_heads, nq, hd), lambda b, j: (0, b * nqb + j, 0)),
                  pl.BlockSpec((n_idx_heads, nq), lambda b, j: (0, b * nqb + j)),
                  pl.BlockSpec((seq, hd), lambda b, j: (b, 0)),
                  pl.BlockSpec((n_heads, nq, hd), lambda b, j: (0, b * nqb + j, 0)),
                  pl.BlockSpec((seq, n_kv * hd), lambda b, j: (b, 0)),
                  pl.BlockSpec((seq, n_kv * hd), lambda b, j: (b, 0))],
        out_specs=pl.BlockSpec((nq, n_heads * hd), lambda b, j: (b * nqb + j, 0)),
        out_shape=jax.ShapeDtypeStruct((batch * seq, n_heads * hd), BF16),
        scratch_shapes=[pltpu.VMEM((seq, nq), I32), pltpu.VMEM((n_kv, 1, group * nq), F32),
                        pltpu.VMEM((n_kv, 1, group * nq), F32), pltpu.VMEM((n_kv, hd, group * nq), F32)],
        compiler_params=_cparams("arbitrary", "arbitrary"),
        name="prompt_attention",
    )(iq_hm, iw_t, ik, q_hm, k, v)


def _attn_sample_kernel(pt_ref, iq_ref, iw_ref, q_ref, ikn_ref, kn_ref, vn_ref, *rest,
                        n_pages, page, t_new, k_sel, sm_scale):
    del pt_ref
    ik_pages = rest[:n_pages]
    k_pages = rest[n_pages:2 * n_pages]
    v_pages = rest[2 * n_pages:3 * n_pages]
    o_ref = rest[3 * n_pages]
    keys_ref, ikb_ref, kb_ref, vb_ref = rest[3 * n_pages + 1:3 * n_pages + 5]
    past = n_pages * page
    new_rows = ikn_ref.shape[0]
    nrows = past + new_rows
    hd = kn_ref.shape[1]
    n_kv = kn_ref.shape[0] // new_rows

    blocks = [(ik_pages[p], k_pages[p], v_pages[p], p * page, page) for p in range(n_pages)]
    blocks.append((ikn_ref, kn_ref, vn_ref, past, new_rows))
    for ik_blk, k_blk, v_blk, r0, rows in blocks:
        ikb_ref[r0:r0 + rows, :] = ik_blk[...].astype(BF16)
        for kvh in range(n_kv):
            kb_ref[r0:r0 + rows, kvh * hd:(kvh + 1) * hd] = k_blk[pl.ds(kvh, rows, stride=n_kv), :].astype(BF16)
            vb_ref[r0:r0 + rows, kvh * hd:(kvh + 1) * hd] = v_blk[pl.ds(kvh, rows, stride=n_kv), :].astype(BF16)

    lane_in = lax.broadcasted_iota(I32, (LANES, LANES), 0) % t_new
    lane_out = lax.broadcasted_iota(I32, (LANES, LANES), 1) % t_new
    head_sum = jnp.where(lane_in == lane_out, 1.0, 0.0).astype(BF16)
    d = lax.dot_general(ikb_ref[...], iq_ref[...], NT_DIMS, preferred_element_type=F32)
    part = jnp.maximum(d, 0.0) * iw_ref[...]
    score = jnp.zeros((nrows, LANES), F32)
    for _ in range(3):
        piece = part.astype(BF16)
        score = score + jnp.dot(piece, head_sum, preferred_element_type=F32)
        part = part - piece.astype(F32)
    new_r = lax.broadcasted_iota(I32, (nrows, LANES), 0) - past
    lane_q = lax.broadcasted_iota(I32, (nrows, LANES), 1) % t_new
    keys_ref[0:nrows, :] = _sortable_key(jnp.where(new_r <= lane_q, score, -jnp.inf))
    keys_ref[nrows:, :] = jnp.full((keys_ref.shape[0] - nrows, LANES), NEG_INF_KEY, I32)

    thr = _topk_threshold(keys_ref, 1, keys_ref.shape[0], k_sel, LANES, keys_ref.shape[0], period=t_new)

    s = lax.dot_general(kb_ref[...], q_ref[...], NT_DIMS, preferred_element_type=F32)
    s = jnp.where(keys_ref[0:nrows, :] >= thr, s * sm_scale, NEG_BIG)
    m = jnp.max(s, axis=0, keepdims=True)
    pr = jnp.exp(s - m).astype(BF16)
    lsum = lax.dot_general(pr, jnp.ones((nrows, LANES), BF16), TN_DIMS, preferred_element_type=F32)
    out = lax.dot_general(pr, vb_ref[...], TN_DIMS, preferred_element_type=F32)
    o_ref[...] = out / jnp.concatenate([lsum] * n_kv, axis=1)


def sample_attention(iq_rows, iw_row, q_blk, ik_new, k_new, v_new, cache_idx_k, cache_k, cache_v,
                     page_table, t_new, k_sel):
    bd, n_pages = page_table.shape
    page = cache_idx_k.shape[1]
    idx_dim = cache_idx_k.shape[2]
    n_kv, hd = cache_k.shape[2], cache_k.shape[3]
    kvw = n_kv * hd
    qw = q_blk.shape[1]
    assert qw == LANES and hd == LANES
    new_rows = ik_new.shape[1]
    nrows = n_pages * page + new_rows
    cache_k = cache_k.reshape(-1, page * n_kv, hd)
    cache_v = cache_v.reshape(-1, page * n_kv, hd)
    k_new = k_new.reshape(bd, new_rows * n_kv, hd)
    v_new = v_new.reshape(bd, new_rows * n_kv, hd)

    def page_spec(rows, p):
        return pl.BlockSpec((None, rows, hd), lambda b, pt: (pt[b * n_pages + p], 0, 0))

    def per_seq(shape):
        return pl.BlockSpec((None,) + shape, lambda b, pt: (b,) + (0,) * len(shape))

    in_specs = [per_seq((LANES, idx_dim)), per_seq((1, LANES)), per_seq((qw, kvw)),
                per_seq((new_rows, idx_dim)), per_seq((new_rows * n_kv, hd)), per_seq((new_rows * n_kv, hd))]
    in_specs += [page_spec(page, p) for p in range(n_pages)]
    in_specs += [page_spec(page * n_kv, p) for p in range(n_pages)]
    in_specs += [page_spec(page * n_kv, p) for p in range(n_pages)]
    kern = functools.partial(_attn_sample_kernel, n_pages=n_pages, page=page, t_new=t_new, k_sel=k_sel,
                             sm_scale=LANES ** -0.5)
    return pl.pallas_call(
        kern,
        grid_spec=pltpu.PrefetchScalarGridSpec(
            num_scalar_prefetch=1,
            grid=(bd,),
            in_specs=in_specs,
            out_specs=pl.BlockSpec((None, qw, kvw), lambda b, pt: (b, 0, 0)),
            scratch_shapes=[pltpu.VMEM((-(-nrows // 64) * 64, LANES), I32),
                            pltpu.VMEM((nrows, idx_dim), BF16),
                            pltpu.VMEM((nrows, kvw), BF16), pltpu.VMEM((nrows, kvw), BF16)]),
        out_shape=jax.ShapeDtypeStruct((bd, qw, kvw), F32),
        compiler_params=_cparams("arbitrary"),
        name="sample_attention",
    )(page_table.reshape(-1), iq_rows, iw_row, q_blk, ik_new, k_new, v_new,
      *([cache_idx_k] * n_pages), *([cache_k] * n_pages), *([cache_v] * n_pages))


def _pool_prompt_kernel(p_ref, halo_ref, w_ref, ps_ref, o_ref, ext_ref, *, tm):
    sblk = pl.program_id(1)
    c = p_ref.shape[-1]
    gc = c // len(POOL_WINDOWS)
    od = o_ref.shape[-1] // len(POOL_WINDOWS)
    ext_ref[0:POOL_HALO, :] = jnp.where(sblk > 0, halo_ref[...], 0.0)
    ext_ref[POOL_HALO:POOL_HALO + tm, :] = p_ref[...]
    pos = sblk * tm + lax.broadcasted_iota(I32, (tm, gc), 0)
    for g, w in enumerate(POOL_WINDOWS):
        cols = slice(g * gc, (g + 1) * gc)
        tot = ext_ref[POOL_HALO:POOL_HALO + tm, cols]
        for dlt in range(1, w):
            tot = tot + ext_ref[POOL_HALO - dlt:POOL_HALO - dlt + tm, cols]
        cnt = jnp.minimum(pos + 1, w).astype(F32)
        pooled = tot / cnt - p_ref[:, cols]
        out = jnp.dot(pooled.astype(BF16), w_ref[g], preferred_element_type=F32)
        o_ref[:, g * od:(g + 1) * od] = (out * ps_ref[:, g * od:(g + 1) * od]).astype(o_ref.dtype)


def pool_prompt(p_in, w_pool, pool_scale, tm=512):
    b, s, c = p_in.shape
    g, gc, od = w_pool.shape
    tm = min(tm, s)
    nsb = s // tm
    hb = tm // POOL_HALO
    return pl.pallas_call(
        functools.partial(_pool_prompt_kernel, tm=tm),
        grid=(b, nsb),
        in_specs=[pl.BlockSpec((None, tm, c), lambda bi, si: (bi, si, 0)),
                  pl.BlockSpec((None, POOL_HALO, c), lambda bi, si: (bi, jnp.maximum(si * hb - 1, 0), 0)),
                  pl.BlockSpec((g, gc, od), lambda bi, si: (0, 0, 0)),
                  pl.BlockSpec((1, g * od), lambda bi, si: (0, 0))],
        out_specs=pl.BlockSpec((tm, g * od), lambda bi, si: (bi * nsb + si, 0)),
        out_shape=jax.ShapeDtypeStruct((b * s, g * od), F32),
        scratch_shapes=[pltpu.VMEM((POOL_HALO + tm, c), F32)],
        compiler_params=_cparams("arbitrary", "arbitrary"),
        name="pool_prompt",
    )(p_in, p_in, w_pool, pool_scale.reshape(1, g * od))


def _pool_sample_kernel(pad_ref, w_ref, ps_ref, o_ref, *, t_new, past):
    n_state = pad_ref.shape[0] - t_new
    c = pad_ref.shape[-1]
    gc = c // len(POOL_WINDOWS)
    od = o_ref.shape[-1] // len(POOL_WINDOWS)
    for t in range(t_new):
        for g, w in enumerate(POOL_WINDOWS):
            cols = slice(g * gc, (g + 1) * gc)
            cur = pad_ref[n_state + t, :, cols]
            tot = cur
            for dlt in range(1, w):
                tot = tot + pad_ref[n_state + t - dlt, :, cols]
            pooled = tot / float(min(past + t + 1, w)) - cur
            out = jnp.dot(pooled.astype(BF16), w_ref[g], preferred_element_type=F32)
            o_ref[t, :, g * od:(g + 1) * od] = out * ps_ref[:, g * od:(g + 1) * od]


def pool_sample(pad_t, w_pool, pool_scale, t_new, past):
    _, bd, c = pad_t.shape
    g, gc, od = w_pool.shape
    return pl.pallas_call(
        functools.partial(_pool_sample_kernel, t_new=t_new, past=past),
        out_shape=jax.ShapeDtypeStruct((t_new, bd, g * od), F32),
        compiler_params=pltpu.CompilerParams(vmem_limit_bytes=VMEM_LIMIT),
        name="pool_sample",
    )(pad_t, w_pool, pool_scale.reshape(1, g * od))


def _merge_kernel(a_ref, w_ref, ga_ref, gb_ref, pool_ref, o_ref):
    acc = jnp.dot(a_ref[...], w_ref[...], preferred_element_type=F32)
    o_ref[...] = (ga_ref[...].astype(F32) * acc + gb_ref[...].astype(F32) * pool_ref[...]).astype(o_ref.dtype)


def merge_branches(attn, w_attn_proj, sig_a, sig_b, pool_out, tm=1024, tn=512):
    n, k = attn.shape
    d = w_attn_proj.shape[1]
    tm, tn = min(tm, n), min(tn, d)
    blk = pl.BlockSpec((tm, tn), lambda i, j: (i, j))
    return pl.pallas_call(
        _merge_kernel,
        grid=(n // tm, d // tn),
        in_specs=[pl.BlockSpec((tm, k), lambda i, j: (i, 0)),
                  pl.BlockSpec((k, tn), lambda i, j: (0, j)), blk, blk, blk],
        out_specs=blk,
        out_shape=jax.ShapeDtypeStruct((n, d), BF16),
        compiler_params=_cparams("arbitrary", "arbitrary"),
        name="merge_branches",
    )(attn, w_attn_proj, sig_a, sig_b, pool_out)


def _out_proj_kernel(m_ref, w_ref, x_ref, gate_ref, o_ref):
    acc = jnp.dot(m_ref[...], w_ref[...], preferred_element_type=F32)
    o_ref[...] = x_ref[...] + gate_ref[...] * acc


def out_project(merged, w_out, x, gate, rows_per_group, tm=1024, tn=512):
    n, k = merged.shape
    d = w_out.shape[1]
    tm, tn = min(tm, n), min(tn, d)
    if gate.ndim == 3:
        tm = min(tm, rows_per_group)
        bpg = rows_per_group // tm
        gate_spec = pl.BlockSpec((None, 1, tn), lambda i, j: (i // bpg, 0, j))
    else:
        gate_spec = pl.BlockSpec((tm, tn), lambda i, j: (i, j))
    blk = pl.BlockSpec((tm, tn), lambda i, j: (i, j))
    return pl.pallas_call(
        _out_proj_kernel,
        grid=(n // tm, d // tn),
        in_specs=[pl.BlockSpec((tm, k), lambda i, j: (i, 0)),
                  pl.BlockSpec((k, tn), lambda i, j: (0, j)), blk, gate_spec],
        out_specs=blk,
        out_shape=jax.ShapeDtypeStruct((n, d), F32),
        compiler_params=_cparams("arbitrary", "arbitrary"),
        name="out_project",
    )(merged, w_out, x, gate)


def _extract_top(work, order, n_top, on_hit, exact):
    vals = []
    for r in range(n_top):
        m = jnp.max(work, axis=0, keepdims=True)
        hit = work == m
        if exact:
            first = jnp.min(jnp.where(hit, order, float(2 ** 20)), axis=0, keepdims=True)
            hit = order == first
        work = jnp.where(hit, -jnp.inf, work)
        on_hit(r, hit)
        vals.append(m)
    return vals


def _top_lists(s, n_top, exact):
    rows, t = s.shape
    order = lax.broadcasted_iota(I32, (rows, t), 0).astype(F32)
    state = {"rank": jnp.full((rows, t), float(rows), F32)}

    def on_hit(r, hit):
        state["rank"] = jnp.where(hit, float(r), state["rank"])

    vals = _extract_top(s, order, n_top, on_hit, exact)
    return jnp.concatenate(vals, axis=0), state["rank"]


def _peer_route_kernel(pq_ref, keys_ref, r2_ref, e2_ref, n1_ref, e1_ref, *, n_heads, n_top):
    t = pq_ref.shape[1]
    pieces = []
    for a in range(n_top):
        nb = n_top // (a + 1)
        rows = -(-nb // 8) * 8
        pieces.append((a, nb, rows))
    flat_idx = jnp.concatenate(
        [a * n_top + lax.broadcasted_iota(I32, (rows, t), 0) for a, _, rows in pieces], axis=0).astype(F32)
    n_cand = flat_idx.shape[0]

    def route(h, exact):
        s1 = lax.dot_general(keys_ref[2 * h], pq_ref[2 * h], NT_DIMS, preferred_element_type=F32)
        s2 = lax.dot_general(keys_ref[2 * h + 1], pq_ref[2 * h + 1], NT_DIMS, preferred_element_type=F32)
        v1, rank1 = _top_lists(s1, n_top, exact)
        v2, rank2 = _top_lists(s2, n_top, exact)
        cand = jnp.concatenate(
            [jnp.where(lax.broadcasted_iota(I32, (rows, t), 0) < nb, v1[a:a + 1] + v2[:rows], -jnp.inf)
             for a, nb, rows in pieces], axis=0)
        state = {"chosen": jnp.zeros((n_cand, t), F32)}

        def on_hit(_, hit):
            state["chosen"] = jnp.where(hit, 1.0, state["chosen"])

        top_vals = _extract_top(cand, flat_idx, n_top, on_hit, exact)
        chosen = state["chosen"]
        z = sum(jnp.exp(m - top_vals[0]) for m in top_vals)
        n1 = jnp.zeros_like(rank1)
        r0 = 0
        for a, _, rows in pieces:
            n_a = jnp.sum(chosen[r0:r0 + rows], axis=0, keepdims=True)
            n1 = jnp.where(rank1 == float(a), n_a, n1)
            r0 += rows
        inv_z = 1.0 / z
        r2_ref[h] = rank2.astype(r2_ref.dtype)
        e2_ref[h] = jnp.where(rank2 < float(n_top), jnp.exp(s2 - v2[0:1]), 0.0).astype(e2_ref.dtype)
        n1_ref[h] = n1
        e1_ref[h] = jnp.where(rank1 < float(n_top), jnp.exp(s1 - v1[0:1]), 0.0) * inv_z
        taken = [jnp.sum(jnp.where(rank < float(n_top), 1.0, 0.0), axis=0, keepdims=True) for rank in (rank1, rank2)]
        taken.append(jnp.sum(chosen, axis=0, keepdims=True))
        return sum(jnp.abs(c - float(n_top)) for c in taken)

    def route_head(h, carry):
        excess = route(h, exact=False)

        @pl.when(jnp.max(excess) > 0.0)
        def _():
            route(h, exact=True)

        return carry

    lax.fori_loop(0, n_heads, route_head, 0)


def peer_route(pq_hm, keys_flat, n_heads, tp=256):
    n = pq_hm.shape[1]
    nk, half = keys_flat.shape[1], keys_flat.shape[2]
    tp = min(tp, n)
    out_spec = pl.BlockSpec((n_heads, nk, tp), lambda i: (0, 0, i))
    return pl.pallas_call(
        functools.partial(_peer_route_kernel, n_heads=n_heads, n_top=PEER_TOPK),
        grid=(n // tp,),
        in_specs=[pl.BlockSpec((2 * n_heads, tp, half), lambda i: (0, i, 0)),
                  pl.BlockSpec(keys_flat.shape, lambda i: (0, 0, 0))],
        out_specs=[out_spec] * 4,
        out_shape=[jax.ShapeDtypeStruct((n_heads, nk, n), dt) for dt in (BF16, BF16, F32, F32)],
        compiler_params=_cparams("arbitrary"),
        name="peer_route",
    )(pq_hm, keys_flat)


def _peer_expert_kernel(ht_ref, u_ref, vt_ref, r2_ref, e2_ref, n1_ref, e1_ref, x_ref, gate_ref, o_ref,
                        a_ref, w_ref, acc_ref, n1s_ref, e1s_ref, *, n_heads, nk):
    e = pl.program_id(1)
    eb, tm = a_ref.shape
    n_sub = eb // nk

    @pl.when(e == 0)
    def _():
        acc_ref[...] = jnp.zeros_like(acc_ref)

    a_ref[...] = jnp.dot(u_ref[...], ht_ref[...], preferred_element_type=F32)

    for h in range(n_heads):
        for s in range(n_sub):
            n1s_ref[h, s] = jnp.broadcast_to(n1_ref[h, s:s + 1, :], (ROW_BLOCK, tm)).astype(BF16)
            e1s_ref[h, s] = jnp.broadcast_to(e1_ref[h, s:s + 1, :], (ROW_BLOCK, tm)).astype(BF16)

    def sub_block(s, carry):
        r0 = pl.multiple_of(s * nk, nk)
        for rb in range(nk // ROW_BLOCK):
            rr = slice(rb * ROW_BLOCK, (rb + 1) * ROW_BLOCK)
            gsum = jnp.zeros((ROW_BLOCK, tm), BF16)
            for h in range(n_heads):
                keep = r2_ref[h, rr, :] < n1s_ref[h, s]
                gsum = gsum + jnp.where(keep, e2_ref[h, rr, :], jnp.zeros((), BF16)) * e1s_ref[h, s]
            a = a_ref[pl.ds(r0 + rb * ROW_BLOCK, ROW_BLOCK), :]
            act = 0.5 * a * (1.0 + lax.erf(a * (2.0 ** -0.5)))
            w_ref[pl.ds(r0 + rb * ROW_BLOCK, ROW_BLOCK), :] = gsum * act.astype(BF16)
        return carry

    lax.fori_loop(0, n_sub, sub_block, 0)
    acc_ref[...] += jnp.dot(vt_ref[...], w_ref[...], preferred_element_type=F32)

    @pl.when(e == pl.num_programs(1) - 1)
    def _():
        o_ref[...] = x_ref[...] + gate_ref[...] * acc_ref[...].T


def peer_experts(h2_t, u, v_t, r2, e2, n1, e1, x, gate, rows_per_group, tm=512, eb=1024):
    d, n = h2_t.shape
    n_exp = u.shape[0]
    n_heads, nk = r2.shape[0], r2.shape[1]
    tm, eb = min(tm, n), min(eb, n_exp)
    assert tm % LANES == 0 and eb % nk == 0 and (eb // nk) % 8 == 0 and nk % ROW_BLOCK == 0
    n_sub = eb // nk
    tok_tile = pl.BlockSpec((n_heads, nk, tm), lambda i, e: (0, 0, i))
    i1_tile = pl.BlockSpec((n_heads, n_sub, tm), lambda i, e: (0, e, i))
    return pl.pallas_call(
        functools.partial(_peer_expert_kernel, n_heads=n_heads, nk=nk),
        grid=(n // tm, n_exp // eb),
        in_specs=[pl.BlockSpec((d, tm), lambda i, e: (0, i)),
                  pl.BlockSpec((eb, d), lambda i, e: (e, 0)),
                  pl.BlockSpec((d, eb), lambda i, e: (0, e)),
                  tok_tile, tok_tile, i1_tile, i1_tile,
                  pl.BlockSpec((tm, d), lambda i, e: (i, 0)),
                  _mod_spec(gate, tm, rows_per_group)],
        out_specs=pl.BlockSpec((tm, d), lambda i, e: (i, 0)),
        out_shape=jax.ShapeDtypeStruct((n, d), F32),
        scratch_shapes=[pltpu.VMEM((eb, tm), F32), pltpu.VMEM((eb, tm), BF16), pltpu.VMEM((d, tm), F32),
                        pltpu.VMEM((n_heads, n_sub, ROW_BLOCK, tm), BF16),
                        pltpu.VMEM((n_heads, n_sub, ROW_BLOCK, tm), BF16)],
        compiler_params=_cparams("arbitrary", "arbitrary"),
        name="peer_experts",
    )(h2_t, u, v_t, r2, e2, n1, e1, x, gate)


def _rope_tables(pos, half):
    freqs = ROPE_THETA ** (-jnp.arange(half, dtype=F32) / half)
    ang = pos.astype(F32)[:, None] * freqs[None, :]
    cos, sin = jnp.cos(ang), jnp.sin(ang)
    return jnp.concatenate([cos, cos], axis=1), jnp.concatenate([-sin, sin], axis=1)


def _channel_mixer(x_mid, shift2, scale2, gate2, rows_per_group, norm2_g, w_peer_q, keys_flat, peer_u, peer_vt,
                   n_peer_heads, tm):
    h2, h2_t = norm_modulate(x_mid, norm2_g, scale2, shift2, rows_per_group, tm, with_transpose=True)
    (pq_hm,) = project(h2, w_peer_q, 0, w_peer_q.shape[1], outs=(("head_major", BF16),))
    r2, e2, n1, e1 = peer_route(pq_hm, keys_flat, n_peer_heads)
    return peer_experts(h2_t, peer_u, peer_vt, r2, e2, n1, e1, x_mid, gate2, rows_per_group, tm=tm)


def kernel(x_prompt, x_sample, cache_k, cache_v, cache_idx_k, state_pool, page_table, c_prompt, c_sample,
           norm1_g, norm2_g, w_mod, b_mod, w_in, q_norm_g, k_norm_g, w_attn_proj, w_pool, pool_scale,
           w_out, w_peer_q, peer_keys, peer_u, peer_v):
    b, s, d = x_prompt.shape
    bd, t_new, _ = x_sample.shape
    hd = q_norm_g.shape[0]
    idx_dim = cache_idx_k.shape[2]
    page = cache_k.shape[1]
    n_kv = cache_k.shape[2]
    n_heads = w_attn_proj.shape[0] // hd
    d_pool = state_pool.shape[2]
    n_state = state_pool.shape[1]
    q_cols, kv_cols = n_heads * hd, n_kv * hd
    idx_heads = (w_in.shape[1] - q_cols - 2 * kv_cols - idx_dim - d_pool - 2 * d) // (idx_dim + 1)
    iq_cols = idx_heads * idx_dim
    past = page_table.shape[1] * page
    n_peer_heads, _, nk, half = peer_keys.shape
    assert hd == LANES and idx_dim == LANES and nk == LANES and half == LANES

    o_ik = q_cols + 2 * kv_cols + iq_cols
    o_iw = o_ik + idx_dim
    iw_pad = jnp.zeros((d, LANES - idx_heads), BF16)
    w_in_c = w_in.astype(BF16)
    w_in_b = jnp.concatenate([w_in_c[:, :o_ik], w_in_c[:, o_iw + idx_heads:], w_in_c[:, o_ik:o_iw],
                              w_in_c[:, o_iw:o_iw + idx_heads], iw_pad], axis=1)
    c_q, c_k, c_v = 0, q_cols, q_cols + kv_cols
    c_iq = c_v + kv_cols
    c_pool = c_iq + iq_cols
    c_ga = c_pool + d_pool
    c_gb = c_ga + d
    c_ik = c_gb + d
    c_iw = c_ik + idx_dim
    w_ap_b = w_attn_proj.astype(BF16)
    w_out_b = w_out.astype(BF16)
    w_pq_b = w_peer_q.astype(BF16)
    w_pool_b = w_pool.astype(BF16)
    keys_flat = peer_keys.reshape(n_peer_heads * 2, nk, half).astype(BF16)
    peer_u_b = peer_u.astype(BF16)
    peer_vt_b = peer_v.astype(BF16).T
    iw_scale = idx_heads ** -0.5 * idx_dim ** -0.5

    mod = modulation(jnp.concatenate([c_prompt, c_sample], axis=0), w_mod, b_mod)
    mod = mod.reshape(b + bd, 6, d)
    mod_p = [mod[:b, i][:, None, :] for i in range(6)]
    mod_s = [jnp.repeat(mod[b:, i], t_new, axis=0) for i in range(6)]

    def mixer_inputs(x2, shift, scale, pos, rows_per_group, tm):
        cos, sin = (jnp.tile(tbl, (x2.shape[0] // pos.shape[0], 1)) for tbl in _rope_tables(pos, hd // 2))
        (h,) = norm_modulate(x2, norm1_g, scale, shift, rows_per_group, tm)
        (q_hm,) = project(h, w_in_b, c_q, q_cols, outs=(("head_major", BF16),), norm_g=q_norm_g, cos=cos, sin=sin, tm=tm)
        k_f, k_b = project(h, w_in_b, c_k, kv_cols, outs=(("token_major", F32), ("token_major", BF16)),
                           norm_g=k_norm_g, cos=cos, sin=sin, tm=tm)
        v_f, v_b = project(h, w_in_b, c_v, kv_cols, outs=(("token_major", F32), ("token_major", BF16)), tm=tm)
        (iq_hm,) = project(h, w_in_b, c_iq, iq_cols, outs=(("head_major", BF16),), cos=cos, sin=sin, tm=tm)
        ik_f, ik_b = project(h, w_in_b, c_ik, idx_dim, outs=(("token_major", F32), ("token_major", BF16)),
                             cos=cos, sin=sin, tm=tm)
        (iw,) = project(h, w_in_b, c_iw, LANES, outs=(("token_major", F32),), scale=iw_scale, tm=tm)
        (p_in,) = project(h, w_in_b, c_pool, d_pool, outs=(("token_major", F32),), tm=tm)
        (sig_a,) = project(h, w_in_b, c_ga, d, outs=(("token_major", BF16),), act="sigmoid", tm=tm)
        (sig_b,) = project(h, w_in_b, c_gb, d, outs=(("token_major", BF16),), act="sigmoid", tm=tm)
        return q_hm, k_f, k_b, v_f, v_b, iq_hm, ik_f, ik_b, iw[:, :idx_heads], p_in, sig_a, sig_b

    xp = x_prompt.reshape(b * s, d)
    tm_p = min(1024, s)
    pos_p = jnp.arange(s)
    (q_hm, k_f, k_b, v_f, v_b, iq_hm, ik_f, ik_b, iw, p_in, sig_a, sig_b) = mixer_inputs(
        xp, mod_p[0], mod_p[1], pos_p, s, tm_p)
    attn_p = prompt_attention(iq_hm, iw.T, ik_b, q_hm, k_b, v_b, b, s, min(TOPK_MAX, s // 4))
    pool_p = pool_prompt(p_in.reshape(b, s, d_pool), w_pool_b, pool_scale)
    merged = merge_branches(attn_p, w_ap_b, sig_a, sig_b, pool_p)
    h_p = out_project(merged, w_out_b, xp, mod_p[2], s)
    y_prompt = _channel_mixer(h_p, mod_p[3], mod_p[4], mod_p[5], s, norm2_g, w_pq_b, keys_flat, peer_u_b,
                              peer_vt_b, n_peer_heads, min(512, s))
    k_prompt = k_f.reshape(b, s, n_kv, hd)
    v_prompt = v_f.reshape(b, s, n_kv, hd)
    idx_k_prompt = ik_f.reshape(b, s, idx_dim)
    pool_state_prompt = p_in.reshape(b, s, d_pool)[:, s - n_state:, :]

    ns = bd * t_new
    xs = x_sample.reshape(ns, d)
    pos_s = past + jnp.arange(t_new)
    (q_hm, k_f, k_b, v_f, v_b, iq_hm, ik_f, ik_b, iw, p_in, sig_a, sig_b) = mixer_inputs(
        xs, mod_s[0], mod_s[1], pos_s, t_new, ns)
    group = n_heads // n_kv
    iq_rows = iq_hm.reshape(idx_heads, bd, t_new, idx_dim).transpose(1, 0, 2, 3).reshape(bd, idx_heads * t_new, idx_dim)
    iq_rows = jnp.pad(iq_rows, ((0, 0), (0, LANES - idx_heads * t_new), (0, 0)))
    iw_row = iw.reshape(bd, t_new, idx_heads).transpose(0, 2, 1).reshape(bd, 1, idx_heads * t_new)
    iw_row = jnp.pad(iw_row, ((0, 0), (0, 0), (0, LANES - idx_heads * t_new)))
    q5 = q_hm.reshape(n_kv, group, bd, t_new, hd).transpose(2, 0, 1, 3, 4)
    eye = jnp.eye(n_kv, dtype=q5.dtype)
    q_blk = (q5[:, :, :, :, None, :] * eye[None, :, None, None, :, None]).reshape(
        bd, n_kv * group * t_new, n_kv * hd)
    q_blk = jnp.pad(q_blk, ((0, 0), (0, LANES - n_kv * group * t_new), (0, 0)))
    new_rows = 16
    def pad_new(a, *tail):
        a = a.reshape((bd, t_new) + tail)
        return jnp.pad(a, ((0, 0), (0, new_rows - t_new)) + ((0, 0),) * len(tail))

    att = sample_attention(iq_rows, iw_row, q_blk, pad_new(ik_f, idx_dim), pad_new(k_f, n_kv, hd),
                           pad_new(v_f, n_kv, hd), cache_idx_k, cache_k, cache_v,
                           page_table, t_new, min(TOPK_MAX, (past + t_new) // 4))
    att = att[:, :n_kv * group * t_new].reshape(bd, n_kv, group, t_new, n_kv, hd)
    att = jnp.einsum('bkgqkd->bqkgd', att).reshape(ns, q_cols).astype(BF16)
    pad_t = jnp.concatenate([state_pool, p_in.reshape(bd, t_new, d_pool)], axis=1)
    pool_s = pool_sample(pad_t.transpose(1, 0, 2), w_pool_b, pool_scale, t_new, past)
    pool_s = pool_s.transpose(1, 0, 2).reshape(ns, d)
    merged = merge_branches(att, w_ap_b, sig_a, sig_b, pool_s)
    h_s = out_project(merged, w_out_b, xs, mod_s[2], t_new)
    y_sample = _channel_mixer(h_s, mod_s[3], mod_s[4], mod_s[5], t_new, norm2_g, w_pq_b, keys_flat, peer_u_b,
                              peer_vt_b, n_peer_heads, ns)

    return (y_prompt.reshape(b, s, d), y_sample.reshape(bd, t_new, d), k_prompt, v_prompt, idx_k_prompt,
            pool_state_prompt, k_f.reshape(bd, t_new, n_kv, hd), v_f.reshape(bd, t_new, n_kv, hd),
            ik_f.reshape(bd, t_new, idx_dim), pad_t[:, t_new:, :])
```

```python
import functools
import math

import jax
import jax.numpy as jnp
from jax import lax
from jax.experimental import pallas as pl
from jax.experimental.pallas import tpu as pltpu

F32 = jnp.float32
BF16 = jnp.bfloat16
I32 = jnp.int32

LANES = 128
EPS = 1e-6
ROPE_THETA = 10000.0
TOPK_MAX = 256
POOL_WINDOWS = (2, 4, 8, 16)
POOL_HALO = 16
PEER_TOPK = 16
ROW_BLOCK = 16
VMEM_LIMIT = 56 * 1024 * 1024

NT_DIMS = (((1,), (1,)), ((), ()))
TN_DIMS = (((0,), (0,)), ((), ()))
NEG_BIG = -1e30
INT_MIN = -2 ** 31


def _cparams(*sem):
    return pltpu.CompilerParams(dimension_semantics=sem, vmem_limit_bytes=VMEM_LIMIT)


def _mod_kernel(c_ref, w_ref, b_ref, o_ref):
    c = c_ref[...]
    a = c * jax.nn.sigmoid(c)
    o_ref[...] = jnp.dot(a, w_ref[...], preferred_element_type=F32,
                         precision=lax.Precision.HIGHEST) + b_ref[...]


def modulation(c, w_mod, b_mod, tn=1024):
    n, d = c.shape
    cols = w_mod.shape[1]
    return pl.pallas_call(
        _mod_kernel,
        grid=(cols // tn,),
        in_specs=[pl.BlockSpec((n, d), lambda j: (0, 0)),
                  pl.BlockSpec((d, tn), lambda j: (0, j)),
                  pl.BlockSpec((1, tn), lambda j: (0, j))],
        out_specs=pl.BlockSpec((n, tn), lambda j: (0, j)),
        out_shape=jax.ShapeDtypeStruct((n, cols), F32),
        compiler_params=_cparams("arbitrary"),
        name="modulation",
    )(c, w_mod, b_mod.reshape(1, cols))


def _norm_mod_kernel(x_ref, g_ref, sc_ref, sh_ref, o_ref, *maybe_ot_ref):
    x = x_ref[...]
    y = x * lax.rsqrt(jnp.mean(x * x, axis=-1, keepdims=True) + EPS) * g_ref[...]
    h = y * (1.0 + sc_ref[...]) + sh_ref[...]
    o_ref[...] = h.astype(o_ref.dtype)
    for ot_ref in maybe_ot_ref:
        ot_ref[...] = h.T.astype(ot_ref.dtype)


def _mod_spec(arr, tm, rows_per_group):
    if arr.ndim == 3:
        bpg = rows_per_group // tm
        return pl.BlockSpec((None, 1, arr.shape[-1]), lambda i, *_: (i // bpg, 0, 0))
    return pl.BlockSpec((tm, arr.shape[-1]), lambda i, *_: (i, 0))


def norm_modulate(x, g, scale, shift, rows_per_group, tm, with_transpose=False):
    n, d = x.shape
    out_specs = [pl.BlockSpec((tm, d), lambda i: (i, 0))]
    out_shape = [jax.ShapeDtypeStruct((n, d), BF16)]
    if with_transpose:
        out_specs.append(pl.BlockSpec((d, tm), lambda i: (0, i)))
        out_shape.append(jax.ShapeDtypeStruct((d, n), BF16))
    return pl.pallas_call(
        _norm_mod_kernel,
        grid=(n // tm,),
        in_specs=[pl.BlockSpec((tm, d), lambda i: (i, 0)),
                  pl.BlockSpec((1, d), lambda i: (0, 0)),
                  _mod_spec(scale, tm, rows_per_group),
                  _mod_spec(shift, tm, rows_per_group)],
        out_specs=out_specs,
        out_shape=out_shape,
        compiler_params=_cparams("arbitrary"),
        name="norm_modulate",
    )(x, g.reshape(1, d), scale, shift)


def _proj_kernel(*refs, norm, rope, act, scale, outs):
    h_ref, w_ref = refs[0], refs[1]
    pos = 2
    g_ref = cos_ref = sin_ref = None
    if norm:
        g_ref = refs[pos]
        pos += 1
    if rope:
        cos_ref, sin_ref = refs[pos], refs[pos + 1]
        pos += 2
    out_refs = refs[pos:]
    acc = jnp.dot(h_ref[...], w_ref[...], preferred_element_type=F32)
    tn = acc.shape[1]
    if norm or rope:
        slabs = []
        for u in range(tn // LANES):
            t = acc[:, u * LANES:(u + 1) * LANES]
            if norm:
                msq = jnp.dot((t * t).astype(BF16), jnp.full((LANES, LANES), 1.0 / LANES, BF16),
                              preferred_element_type=F32)
                t = t * lax.rsqrt(msq + EPS) * g_ref[...]
            if rope:
                t = t * cos_ref[...] + pltpu.roll(t, LANES // 2, 1) * sin_ref[...]
            slabs.append(t)
    else:
        if scale != 1.0:
            acc = acc * scale
        if act == "sigmoid":
            acc = jax.nn.sigmoid(acc)
        slabs = None
    for kind, o_ref in zip(outs, out_refs):
        if kind == "head_major":
            for u in range(tn // LANES):
                t = acc[:, u * LANES:(u + 1) * LANES] if slabs is None else slabs[u]
                o_ref[u] = t.astype(o_ref.dtype)
        elif slabs is not None:
            for u, t in enumerate(slabs):
                o_ref[:, u * LANES:(u + 1) * LANES] = t.astype(o_ref.dtype)
        else:
            o_ref[...] = acc.astype(o_ref.dtype)


def project(h, w, col0, ncols, *, outs, norm_g=None, cos=None, sin=None, act="none", scale=1.0,
            tm=1024, tn=2048):
    n, k = h.shape
    tn = min(tn, ncols)
    while col0 % tn or ncols % tn:
        tn //= 2
    tm = min(tm, n)
    assert tn % LANES == 0 and n % tm == 0
    c0 = col0 // tn
    in_specs = [pl.BlockSpec((tm, k), lambda i, j: (i, 0)),
                pl.BlockSpec((k, tn), lambda i, j: (0, j + c0))]
    args = [h, w]
    if norm_g is not None:
        in_specs.append(pl.BlockSpec((1, LANES), lambda i, j: (0, 0)))
        args.append(norm_g.reshape(1, LANES))
    if cos is not None:
        in_specs += [pl.BlockSpec((tm, LANES), lambda i, j: (i, 0))] * 2
        args += [cos, sin]
    out_specs, out_shapes = [], []
    for kind, dt in outs:
        if kind == "head_major":
            out_specs.append(pl.BlockSpec((tn // LANES, tm, LANES), lambda i, j: (j, i, 0)))
            out_shapes.append(jax.ShapeDtypeStruct((ncols // LANES, n, LANES), dt))
        else:
            out_specs.append(pl.BlockSpec((tm, tn), lambda i, j: (i, j)))
            out_shapes.append(jax.ShapeDtypeStruct((n, ncols), dt))
    res = pl.pallas_call(
        functools.partial(_proj_kernel, norm=norm_g is not None, rope=cos is not None, act=act,
                          scale=scale, outs=tuple(kd for kd, _ in outs)),
        grid=(n // tm, ncols // tn),
        in_specs=in_specs,
        out_specs=out_specs,
        out_shape=out_shapes,
        compiler_params=_cparams("arbitrary", "arbitrary"),
        name="project",
    )(*args)
    return res


def _sortable_key(score):
    bits = pltpu.bitcast(score, I32)
    return jnp.where(bits < 0, bits ^ jnp.int32(0x7FFFFFFF), bits)


NEG_INF_KEY = (-8388608) ^ 0x7FFFFFFF


def _lane_group_sum(x, period):
    shift = period
    while shift < LANES:
        x = x + pltpu.roll(x, shift, 1)
        shift *= 2
    return x


def _topk_threshold(keys_ref, nchunks, ck, k, width, max_rows, period=None):
    acc_rows = math.gcd(ck, 64)

    def count(pred):
        def body(c, cnt):
            r0 = pl.multiple_of(c * ck, 8)
            kk = keys_ref[pl.ds(r0, ck), :]
            rows = r0 + lax.broadcasted_iota(I32, kk.shape, 0)
            hit = jnp.where(pred(kk, rows), 1.0, 0.0)
            return cnt + jnp.sum(hit.reshape(ck // acc_rows, acc_rows, width), axis=0)
        part = lax.fori_loop(0, nchunks, body, jnp.zeros((acc_rows, width), F32))
        return jnp.sum(part, axis=0, keepdims=True)

    if period is None:
        def bit_step(i, ut):
            uc = ut | jnp.left_shift(jnp.int32(1), jnp.int32(31) - i)
            cand = uc ^ jnp.int32(INT_MIN)
            return jnp.where(count(lambda kk, _: kk >= cand) >= float(k), uc, ut)
        ut = lax.fori_loop(0, 32, bit_step, jnp.zeros((1, width), I32))
    else:
        replica = lax.broadcasted_iota(I32, (1, width), 1) // period
        bits_per_pass = (width // period).bit_length() - 1
        same_col = jnp.where(lax.broadcasted_iota(I32, (width, width), 0) % period
                             == lax.broadcasted_iota(I32, (width, width), 1) % period, 1.0, 0.0).astype(BF16)
        ut = jnp.zeros((1, width), I32)
        hi = 32
        while hi > 0:
            nb = min(bits_per_pass, hi)
            lo = hi - nb
            cand = (ut | jnp.left_shift(replica, lo)) ^ jnp.int32(INT_MIN)
            ok = jnp.where(count(lambda kk, _, cand=cand: kk >= cand) >= float(k), 1.0, 0.0)
            ok = jnp.where(replica < 2 ** nb, ok, 0.0)
            n_ok = jnp.dot(jnp.broadcast_to(ok, (8, width)).astype(BF16), same_col,
                           preferred_element_type=F32)[0:1]
            best = n_ok.astype(I32) - 1
            ut = ut | jnp.left_shift(best, lo)
            hi = lo
    thr = ut ^ jnp.int32(INT_MIN)

    real = thr > jnp.int32(NEG_INF_KEY)
    surplus = jnp.where(real, count(lambda kk, _: kk >= thr) - float(k), 0.0)

    @pl.when(jnp.max(surplus) > 0.0)
    def _():
        need = float(k) - count(lambda kk, _: kk > thr)
        n_bits = max(1, (max_rows - 1).bit_length())

        def idx_step(i, jv):
            cand = jv | jnp.left_shift(jnp.int32(1), jnp.int32(n_bits - 1) - i)
            below = count(lambda kk, rows: jnp.where(kk == thr, rows, jnp.int32(2 ** 30)) < cand)
            return jnp.where(below < need, cand, jv)
        last = lax.fori_loop(0, n_bits, idx_step, jnp.zeros((1, width), I32))
        last = jnp.where(surplus > 0.0, last, jnp.int32(2 ** 30))

        def demote(c, carry):
            r0 = pl.multiple_of(c * ck, 8)
            kk = keys_ref[pl.ds(r0, ck), :]
            rows = r0 + lax.broadcasted_iota(I32, kk.shape, 0)
            keys_ref[pl.ds(r0, ck), :] = jnp.where(jnp.where(kk == thr, rows, 0) > last, thr - 1, kk)
            return carry
        lax.fori_loop(0, nchunks, demote, 0)

    return jnp.maximum(thr, jnp.int32(NEG_INF_KEY + 1))


def _attn_prompt_kernel(iq_ref, iw_ref, ik_ref, q_ref, k_ref, v_ref, o_ref, keys_ref, m_ref, l_ref, acc_ref, *,
                        nq, ck, k_sel, n_idx_heads, n_kv, group, sm_scale):
    j = pl.program_id(1)
    nchunks = (j * nq) // ck + 1
    hd = LANES

    qpos = j * nq + lax.broadcasted_iota(I32, (ck, nq), 1)

    def score_chunk(c, carry):
        r0 = pl.multiple_of(c * ck, ck)
        ikc = ik_ref[pl.ds(r0, ck), :]
        acc = jnp.zeros((ck, nq), F32)
        for hp in range(n_idx_heads // 2):
            rhs = iq_ref[2 * hp:2 * hp + 2].reshape(2 * nq, hd)
            d = lax.dot_general(ikc, rhs, NT_DIMS, preferred_element_type=F32)
            for u in range(2):
                h = 2 * hp + u
                acc = acc + jnp.maximum(d[:, u * nq:(u + 1) * nq], 0.0) * iw_ref[h:h + 1, :]
        kpos = r0 + lax.broadcasted_iota(I32, (ck, nq), 0)
        score = jnp.where(kpos <= qpos, acc, -jnp.inf)
        keys_ref[pl.ds(r0, ck), :] = _sortable_key(score)
        return carry

    lax.fori_loop(0, nchunks, score_chunk, 0)

    thr = _topk_threshold(keys_ref, nchunks, ck, k_sel, nq, keys_ref.shape[0])

    gw = group * nq
    exp2_scale = sm_scale * math.log2(math.e)
    m_ref[...] = jnp.full(m_ref.shape, NEG_BIG, F32)
    l_ref[...] = jnp.zeros(l_ref.shape, F32)
    acc_ref[...] = jnp.zeros(acc_ref.shape, F32)

    def attend(c, carry):
        r0 = pl.multiple_of(c * ck, ck)
        sel = keys_ref[pl.ds(r0, ck), :] >= thr
        for kvh in range(n_kv):
            qg = q_ref[group * kvh:group * (kvh + 1)].reshape(gw, hd)
            kc = k_ref[pl.ds(r0, ck), kvh * hd:(kvh + 1) * hd]
            s = lax.dot_general(kc, qg, NT_DIMS, preferred_element_type=F32)
            s = jnp.concatenate([jnp.where(sel, s[:, g * nq:(g + 1) * nq], NEG_BIG) for g in range(group)], axis=1)
            m = m_ref[kvh]
            m_new = jnp.maximum(m, jnp.max(s, axis=0, keepdims=True))
            alpha = jnp.exp2((m - m_new) * exp2_scale)
            p = jnp.exp2((s - m_new) * exp2_scale)
            m_ref[kvh] = m_new
            l_ref[kvh] = alpha * l_ref[kvh] + jnp.sum(p, axis=0, keepdims=True)
            vc = v_ref[pl.ds(r0, ck), kvh * hd:(kvh + 1) * hd]
            pv = lax.dot_general(vc, p.astype(BF16), TN_DIMS, preferred_element_type=F32)
            acc_ref[kvh] = alpha * acc_ref[kvh] + pv
        return carry

    lax.fori_loop(0, nchunks, attend, 0)
    for kvh in range(n_kv):
        out_t = acc_ref[kvh] / l_ref[kvh]
        for g in range(group):
            head = group * kvh + g
            o_ref[:, head * hd:(head + 1) * hd] = out_t[:, g * nq:(g + 1) * nq].T.astype(o_ref.dtype)


def prompt_attention(iq_hm, iw_t, ik, q_hm, k, v, batch, seq, k_sel, nq=128, ck=512):
    n_idx_heads = iq_hm.shape[0]
    n_heads = q_hm.shape[0]
    hd = LANES
    n_kv = k.shape[1] // hd
    group = n_heads // n_kv
    ck = min(ck, seq)
    nqb = seq // nq
    kern = functools.partial(_attn_prompt_kernel, nq=nq, ck=ck, k_sel=k_sel, n_idx_heads=n_idx_heads,
                             n_kv=n_kv, group=group, sm_scale=hd ** -0.5)
    return pl.pallas_call(
        kern,
        grid=(batch, nqb),
        in_specs=[pl.BlockSpec((n_idx_heads, nq, hd), lambda b, j: (0, b * nqb + j, 0)),
                  pl.BlockSpec((n_idx_heads, nq), lambda b, j: (0, b * nqb + j)),
                  pl.BlockSpec((seq, hd), lambda b, j: (b, 0)),
                  pl.BlockSpec((n_heads, nq, hd), lambda b, j: (0, b * nqb + j, 0)),
                  pl.BlockSpec((seq, n_kv * hd), lambda b, j: (b, 0)),
                  pl.BlockSpec((seq, n_kv * hd), lambda b, j: (b, 0))],
        out_specs=pl.BlockSpec((nq, n_heads * hd), lambda b, j: (b * nqb + j, 0)),
        out_shape=jax.ShapeDtypeStruct((batch * seq, n_heads * hd), BF16),
        scratch_shapes=[pltpu.VMEM((seq, nq), I32), pltpu.VMEM((n_kv, 1, group * nq), F32),
                        pltpu.VMEM((n_kv, 1, group * nq), F32), pltpu.VMEM((n_kv, hd, group * nq), F32)],
        compiler_params=_cparams("arbitrary", "arbitrary"),
        name="prompt_attention",
    )(iq_hm, iw_t, ik, q_hm, k, v)


def _attn_sample_kernel(pt_ref, iq_ref, iw_ref, q_ref, ikn_ref, kn_ref, vn_ref, *rest,
                        n_pages, page, t_new, k_sel, sm_scale):
    del pt_ref
    ik_pages = rest[:n_pages]
    k_pages = rest[n_pages:2 * n_pages]
    v_pages = rest[2 * n_pages:3 * n_pages]
    o_ref = rest[3 * n_pages]
    keys_ref, ikb_ref, kb_ref, vb_ref = rest[3 * n_pages + 1:3 * n_pages + 5]
    past = n_pages * page
    new_rows = ikn_ref.shape[0]
    nrows = past + new_rows
    hd = kn_ref.shape[1]
    n_kv = kn_ref.shape[0] // new_rows

    blocks = [(ik_pages[p], k_pages[p], v_pages[p], p * page, page) for p in range(n_pages)]
    blocks.append((ikn_ref, kn_ref, vn_ref, past, new_rows))
    for ik_blk, k_blk, v_blk, r0, rows in blocks:
        ikb_ref[r0:r0 + rows, :] = ik_blk[...].astype(BF16)
        for kvh in range(n_kv):
            kb_ref[r0:r0 + rows, kvh * hd:(kvh + 1) * hd] = k_blk[pl.ds(kvh, rows, stride=n_kv), :].astype(BF16)
            vb_ref[r0:r0 + rows, kvh * hd:(kvh + 1) * hd] = v_blk[pl.ds(kvh, rows, stride=n_kv), :].astype(BF16)

    lane_in = lax.broadcasted_iota(I32, (LANES, LANES), 0) % t_new
    lane_out = lax.broadcasted_iota(I32, (LANES, LANES), 1) % t_new
    head_sum = jnp.where(lane_in == lane_out, 1.0, 0.0).astype(BF16)
    d = lax.dot_general(ikb_ref[...], iq_ref[...], NT_DIMS, preferred_element_type=F32)
    part = jnp.maximum(d, 0.0) * iw_ref[...]
    score = jnp.zeros((nrows, LANES), F32)
    for _ in range(3):
        piece = part.astype(BF16)
        score = score + jnp.dot(piece, head_sum, preferred_element_type=F32)
        part = part - piece.astype(F32)
    new_r = lax.broadcasted_iota(I32, (nrows, LANES), 0) - past
    lane_q = lax.broadcasted_iota(I32, (nrows, LANES), 1) % t_new
    keys_ref[0:nrows, :] = _sortable_key(jnp.where(new_r <= lane_q, score, -jnp.inf))
    keys_ref[nrows:, :] = jnp.full((keys_ref.shape[0] - nrows, LANES), NEG_INF_KEY, I32)

    thr = _topk_threshold(keys_ref, 1, keys_ref.shape[0], k_sel, LANES, keys_ref.shape[0], period=t_new)

    s = lax.dot_general(kb_ref[...], q_ref[...], NT_DIMS, preferred_element_type=F32)
    s = jnp.where(keys_ref[0:nrows, :] >= thr, s * sm_scale, NEG_BIG)
    m = jnp.max(s, axis=0, keepdims=True)
    pr = jnp.exp(s - m).astype(BF16)
    lsum = lax.dot_general(pr, jnp.ones((nrows, LANES), BF16), TN_DIMS, preferred_element_type=F32)
    out = lax.dot_general(pr, vb_ref[...], TN_DIMS, preferred_element_type=F32)
    o_ref[...] = out / jnp.concatenate([lsum] * n_kv, axis=1)


def sample_attention(iq_rows, iw_row, q_blk, ik_new, k_new, v_new, cache_idx_k, cache_k, cache_v,
                     page_table, t_new, k_sel):
    bd, n_pages = page_table.shape
    page = cache_idx_k.shape[1]
    idx_dim = cache_idx_k.shape[2]
    n_kv, hd = cache_k.shape[2], cache_k.shape[3]
    kvw = n_kv * hd
    qw = q_blk.shape[1]
    assert qw == LANES and hd == LANES
    new_rows = ik_new.shape[1]
    nrows = n_pages * page + new_rows
    cache_k = cache_k.reshape(-1, page * n_kv, hd)
    cache_v = cache_v.reshape(-1, page * n_kv, hd)
    k_new = k_new.reshape(bd, new_rows * n_kv, hd)
    v_new = v_new.reshape(bd, new_rows * n_kv, hd)

    def page_spec(rows, p):
        return pl.BlockSpec((None, rows, hd), lambda b, pt: (pt[b * n_pages + p], 0, 0))

    def per_seq(shape):
        return pl.BlockSpec((None,) + shape, lambda b, pt: (b,) + (0,) * len(shape))

    in_specs = [per_seq((LANES, idx_dim)), per_seq((1, LANES)), per_seq((qw, kvw)),
                per_seq((new_rows, idx_dim)), per_seq((new_rows * n_kv, hd)), per_seq((new_rows * n_kv, hd))]
    in_specs += [page_spec(page, p) for p in range(n_pages)]
    in_specs += [page_spec(page * n_kv, p) for p in range(n_pages)]
    in_specs += [page_spec(page * n_kv, p) for p in range(n_pages)]
    kern = functools.partial(_attn_sample_kernel, n_pages=n_pages, page=page, t_new=t_new, k_sel=k_sel,
                             sm_scale=LANES ** -0.5)
    return pl.pallas_call(
        kern,
        grid_spec=pltpu.PrefetchScalarGridSpec(
            num_scalar_prefetch=1,
            grid=(bd,),
            in_specs=in_specs,
            out_specs=pl.BlockSpec((None, qw, kvw), lambda b, pt: (b, 0, 0)),
            scratch_shapes=[pltpu.VMEM((-(-nrows // 64) * 64, LANES), I32),
                            pltpu.VMEM((nrows, idx_dim), BF16),
                            pltpu.VMEM((nrows, kvw), BF16), pltpu.VMEM((nrows, kvw), BF16)]),
        out_shape=jax.ShapeDtypeStruct((bd, qw, kvw), F32),
        compiler_params=_cparams("arbitrary"),
        name="sample_attention",
    )(page_table.reshape(-1), iq_rows, iw_row, q_blk, ik_new, k_new, v_new,
      *([cache_idx_k] * n_pages), *([cache_k] * n_pages), *([cache_v] * n_pages))


def _pool_prompt_kernel(p_ref, halo_ref, w_ref, ps_ref, o_ref, ext_ref, *, tm):
    sblk = pl.program_id(1)
    c = p_ref.shape[-1]
    gc = c // len(POOL_WINDOWS)
    od = o_ref.shape[-1] // len(POOL_WINDOWS)
    ext_ref[0:POOL_HALO, :] = jnp.where(sblk > 0, halo_ref[...], 0.0)
    ext_ref[POOL_HALO:POOL_HALO + tm, :] = p_ref[...]
    pos = sblk * tm + lax.broadcasted_iota(I32, (tm, gc), 0)
    for g, w in enumerate(POOL_WINDOWS):
        cols = slice(g * gc, (g + 1) * gc)
        tot = ext_ref[POOL_HALO:POOL_HALO + tm, cols]
        for dlt in range(1, w):
            tot = tot + ext_ref[POOL_HALO - dlt:POOL_HALO - dlt + tm, cols]
        cnt = jnp.minimum(pos + 1, w).astype(F32)
        pooled = tot / cnt - p_ref[:, cols]
        out = jnp.dot(pooled.astype(BF16), w_ref[g], preferred_element_type=F32)
        o_ref[:, g * od:(g + 1) * od] = (out * ps_ref[:, g * od:(g + 1) * od]).astype(o_ref.dtype)


def pool_prompt(p_in, w_pool, pool_scale, tm=512):
    b, s, c = p_in.shape
    g, gc, od = w_pool.shape
    tm = min(tm, s)
    nsb = s // tm
    hb = tm // POOL_HALO
    return pl.pallas_call(
        functools.partial(_pool_prompt_kernel, tm=tm),
        grid=(b, nsb),
        in_specs=[pl.BlockSpec((None, tm, c), lambda bi, si: (bi, si, 0)),
                  pl.BlockSpec((None, POOL_HALO, c), lambda bi, si: (bi, jnp.maximum(si * hb - 1, 0), 0)),
                  pl.BlockSpec((g, gc, od), lambda bi, si: (0, 0, 0)),
                  pl.BlockSpec((1, g * od), lambda bi, si: (0, 0))],
        out_specs=pl.BlockSpec((tm, g * od), lambda bi, si: (bi * nsb + si, 0)),
        out_shape=jax.ShapeDtypeStruct((b * s, g * od), F32),
        scratch_shapes=[pltpu.VMEM((POOL_HALO + tm, c), F32)],
        compiler_params=_cparams("arbitrary", "arbitrary"),
        name="pool_prompt",
    )(p_in, p_in, w_pool, pool_scale.reshape(1, g * od))


def _pool_sample_kernel(pad_ref, w_ref, ps_ref, o_ref, *, t_new, past):
    n_state = pad_ref.shape[0] - t_new
    c = pad_ref.shape[-1]
    gc = c // len(POOL_WINDOWS)
    od = o_ref.shape[-1] // len(POOL_WINDOWS)
    for t in range(t_new):
        for g, w in enumerate(POOL_WINDOWS):
            cols = slice(g * gc, (g + 1) * gc)
            cur = pad_ref[n_state + t, :, cols]
            tot = cur
            for dlt in range(1, w):
                tot = tot + pad_ref[n_state + t - dlt, :, cols]
            pooled = tot / float(min(past + t + 1, w)) - cur
            out = jnp.dot(pooled.astype(BF16), w_ref[g], preferred_element_type=F32)
            o_ref[t, :, g * od:(g + 1) * od] = out * ps_ref[:, g * od:(g + 1) * od]


def pool_sample(pad_t, w_pool, pool_scale, t_new, past):
    _, bd, c = pad_t.shape
    g, gc, od = w_pool.shape
    return pl.pallas_call(
        functools.partial(_pool_sample_kernel, t_new=t_new, past=past),
        out_shape=jax.ShapeDtypeStruct((t_new, bd, g * od), F32),
        compiler_params=pltpu.CompilerParams(vmem_limit_bytes=VMEM_LIMIT),
        name="pool_sample",
    )(pad_t, w_pool, pool_scale.reshape(1, g * od))


def _merge_kernel(a_ref, w_ref, ga_ref, gb_ref, pool_ref, o_ref):
    acc = jnp.dot(a_ref[...], w_ref[...], preferred_element_type=F32)
    o_ref[...] = (ga_ref[...].astype(F32) * acc + gb_ref[...].astype(F32) * pool_ref[...]).astype(o_ref.dtype)


def merge_branches(attn, w_attn_proj, sig_a, sig_b, pool_out, tm=1024, tn=1024):
    n, k = attn.shape
    d = w_attn_proj.shape[1]
    tm, tn = min(tm, n), min(tn, d)
    blk = pl.BlockSpec((tm, tn), lambda i, j: (i, j))
    return pl.pallas_call(
        _merge_kernel,
        grid=(n // tm, d // tn),
        in_specs=[pl.BlockSpec((tm, k), lambda i, j: (i, 0)),
                  pl.BlockSpec((k, tn), lambda i, j: (0, j)), blk, blk, blk],
        out_specs=blk,
        out_shape=jax.ShapeDtypeStruct((n, d), BF16),
        compiler_params=_cparams("arbitrary", "arbitrary"),
        name="merge_branches",
    )(attn, w_attn_proj, sig_a, sig_b, pool_out)


def _out_proj_kernel(m_ref, w_ref, x_ref, gate_ref, o_ref):
    acc = jnp.dot(m_ref[...], w_ref[...], preferred_element_type=F32)
    o_ref[...] = x_ref[...] + gate_ref[...] * acc


def out_project(merged, w_out, x, gate, rows_per_group, tm=1024, tn=1024):
    n, k = merged.shape
    d = w_out.shape[1]
    tm, tn = min(tm, n), min(tn, d)
    if gate.ndim == 3:
        tm = min(tm, rows_per_group)
        bpg = rows_per_group // tm
        gate_spec = pl.BlockSpec((None, 1, tn), lambda i, j: (i // bpg, 0, j))
    else:
        gate_spec = pl.BlockSpec((tm, tn), lambda i, j: (i, j))
    blk = pl.BlockSpec((tm, tn), lambda i, j: (i, j))
    return pl.pallas_call(
        _out_proj_kernel,
        grid=(n // tm, d // tn),
        in_specs=[pl.BlockSpec((tm, k), lambda i, j: (i, 0)),
                  pl.BlockSpec((k, tn), lambda i, j: (0, j)), blk, gate_spec],
        out_specs=blk,
        out_shape=jax.ShapeDtypeStruct((n, d), F32),
        compiler_params=_cparams("arbitrary", "arbitrary"),
        name="out_project",
    )(merged, w_out, x, gate)


def _extract_top(work, order, n_top, on_hit, exact):
    vals = []
    for r in range(n_top):
        m = jnp.max(work, axis=0, keepdims=True)
        hit = work == m
        if exact:
            first = jnp.min(jnp.where(hit, order, float(2 ** 20)), axis=0, keepdims=True)
            hit = order == first
        work = jnp.where(hit, -jnp.inf, work)
        on_hit(r, hit)
        vals.append(m)
    return vals


def _top_lists(s, n_top, exact):
    rows, t = s.shape
    order = lax.broadcasted_iota(I32, (rows, t), 0).astype(F32)
    state = {"rank": jnp.full((rows, t), float(rows), F32)}

    def on_hit(r, hit):
        state["rank"] = jnp.where(hit, float(r), state["rank"])

    vals = _extract_top(s, order, n_top, on_hit, exact)
    return jnp.concatenate(vals, axis=0), state["rank"]


def _peer_route_kernel(pq_ref, keys_ref, r2_ref, e2_ref, n1_ref, e1_ref, *, n_heads, n_top):
    t = pq_ref.shape[1]
    pieces = []
    for a in range(n_top):
        nb = n_top // (a + 1)
        rows = -(-nb // 8) * 8
        pieces.append((a, nb, rows))
    flat_idx = jnp.concatenate(
        [a * n_top + lax.broadcasted_iota(I32, (rows, t), 0) for a, _, rows in pieces], axis=0).astype(F32)
    n_cand = flat_idx.shape[0]

    def route(h, exact):
        s1 = lax.dot_general(keys_ref[2 * h], pq_ref[2 * h], NT_DIMS, preferred_element_type=F32)
        s2 = lax.dot_general(keys_ref[2 * h + 1], pq_ref[2 * h + 1], NT_DIMS, preferred_element_type=F32)
        v1, rank1 = _top_lists(s1, n_top, exact)
        v2, rank2 = _top_lists(s2, n_top, exact)
        cand = jnp.concatenate(
            [jnp.where(lax.broadcasted_iota(I32, (rows, t), 0) < nb, v1[a:a + 1] + v2[:rows], -jnp.inf)
             for a, nb, rows in pieces], axis=0)
        state = {"chosen": jnp.zeros((n_cand, t), F32)}

        def on_hit(_, hit):
            state["chosen"] = jnp.where(hit, 1.0, state["chosen"])

        top_vals = _extract_top(cand, flat_idx, n_top, on_hit, exact)
        chosen = state["chosen"]
        z = sum(jnp.exp(m - top_vals[0]) for m in top_vals)
        n1 = jnp.zeros_like(rank1)
        r0 = 0
        for a, _, rows in pieces:
            n_a = jnp.sum(chosen[r0:r0 + rows], axis=0, keepdims=True)
            n1 = jnp.where(rank1 == float(a), n_a, n1)
            r0 += rows
        inv_z = 1.0 / z
        r2_ref[h] = rank2.astype(r2_ref.dtype)
        e2_ref[h] = jnp.where(rank2 < float(n_top), jnp.exp(s2 - v2[0:1]), 0.0).astype(e2_ref.dtype)
        n1_ref[h] = n1
        e1_ref[h] = jnp.where(rank1 < float(n_top), jnp.exp(s1 - v1[0:1]), 0.0) * inv_z
        taken = [jnp.sum(jnp.where(rank < float(n_top), 1.0, 0.0), axis=0, keepdims=True) for rank in (rank1, rank2)]
        taken.append(jnp.sum(chosen, axis=0, keepdims=True))
        return sum(jnp.abs(c - float(n_top)) for c in taken)

    def route_head(h, carry):
        excess = route(h, exact=False)

        @pl.when(jnp.max(excess) > 0.0)
        def _():
            route(h, exact=True)

        return carry

    lax.fori_loop(0, n_heads, route_head, 0)


def peer_route(pq_hm, keys_flat, n_heads, tp=256):
    n = pq_hm.shape[1]
    nk, half = keys_flat.shape[1], keys_flat.shape[2]
    tp = min(tp, n)
    out_spec = pl.BlockSpec((n_heads, nk, tp), lambda i: (0, 0, i))
    return pl.pallas_call(
        functools.partial(_peer_route_kernel, n_heads=n_heads, n_top=PEER_TOPK),
        grid=(n // tp,),
        in_specs=[pl.BlockSpec((2 * n_heads, tp, half), lambda i: (0, i, 0)),
                  pl.BlockSpec(keys_flat.shape, lambda i: (0, 0, 0))],
        out_specs=[out_spec] * 4,
        out_shape=[jax.ShapeDtypeStruct((n_heads, nk, n), dt) for dt in (BF16, BF16, F32, F32)],
        compiler_params=_cparams("arbitrary"),
        name="peer_route",
    )(pq_hm, keys_flat)


def _peer_expert_kernel(ht_ref, u_ref, vt_ref, r2_ref, e2_ref, n1_ref, e1_ref, x_ref, gate_ref, o_ref,
                        a_ref, w_ref, acc_ref, n1s_ref, e1s_ref, *, n_heads, nk):
    e = pl.program_id(1)
    eb, tm = a_ref.shape
    n_sub = eb // nk

    @pl.when(e == 0)
    def _():
        acc_ref[...] = jnp.zeros_like(acc_ref)

    a_ref[...] = jnp.dot(u_ref[...], ht_ref[...], preferred_element_type=F32)

    for h in range(n_heads):
        for s in range(n_sub):
            n1s_ref[h, s] = jnp.broadcast_to(n1_ref[h, s:s + 1, :], (ROW_BLOCK, tm)).astype(BF16)
            e1s_ref[h, s] = jnp.broadcast_to(e1_ref[h, s:s + 1, :], (ROW_BLOCK, tm)).astype(BF16)

    def sub_block(s, carry):
        r0 = pl.multiple_of(s * nk, nk)
        for rb in range(nk // ROW_BLOCK):
            rr = slice(rb * ROW_BLOCK, (rb + 1) * ROW_BLOCK)
            gsum = jnp.zeros((ROW_BLOCK, tm), BF16)
            for h in range(n_heads):
                keep = r2_ref[h, rr, :] < n1s_ref[h, s]
                gsum = gsum + jnp.where(keep, e2_ref[h, rr, :], jnp.zeros((), BF16)) * e1s_ref[h, s]
            a = a_ref[pl.ds(r0 + rb * ROW_BLOCK, ROW_BLOCK), :]
            act = 0.5 * a * (1.0 + lax.erf(a * (2.0 ** -0.5)))
            w_ref[pl.ds(r0 + rb * ROW_BLOCK, ROW_BLOCK), :] = gsum * act.astype(BF16)
        return carry

    lax.fori_loop(0, n_sub, sub_block, 0)
    acc_ref[...] += jnp.dot(vt_ref[...], w_ref[...], preferred_element_type=F32)

    @pl.when(e == pl.num_programs(1) - 1)
    def _():
        o_ref[...] = x_ref[...] + gate_ref[...] * acc_ref[...].T


def peer_experts(h2_t, u, v_t, r2, e2, n1, e1, x, gate, rows_per_group, tm=512, eb=1024):
    d, n = h2_t.shape
    n_exp = u.shape[0]
    n_heads, nk = r2.shape[0], r2.shape[1]
    tm, eb = min(tm, n), min(eb, n_exp)
    assert tm % LANES == 0 and eb % nk == 0 and (eb // nk) % 8 == 0 and nk % ROW_BLOCK == 0
    n_sub = eb // nk
    tok_tile = pl.BlockSpec((n_heads, nk, tm), lambda i, e: (0, 0, i))
    i1_tile = pl.BlockSpec((n_heads, n_sub, tm), lambda i, e: (0, e, i))
    return pl.pallas_call(
        functools.partial(_peer_expert_kernel, n_heads=n_heads, nk=nk),
        grid=(n // tm, n_exp // eb),
        in_specs=[pl.BlockSpec((d, tm), lambda i, e: (0, i)),
                  pl.BlockSpec((eb, d), lambda i, e: (e, 0)),
                  pl.BlockSpec((d, eb), lambda i, e: (0, e)),
                  tok_tile, tok_tile, i1_tile, i1_tile,
                  pl.BlockSpec((tm, d), lambda i, e: (i, 0)),
                  _mod_spec(gate, tm, rows_per_group)],
        out_specs=pl.BlockSpec((tm, d), lambda i, e: (i, 0)),
        out_shape=jax.ShapeDtypeStruct((n, d), F32),
        scratch_shapes=[pltpu.VMEM((eb, tm), F32), pltpu.VMEM((eb, tm), BF16), pltpu.VMEM((d, tm), F32),
                        pltpu.VMEM((n_heads, n_sub, ROW_BLOCK, tm), BF16),
                        pltpu.VMEM((n_heads, n_sub, ROW_BLOCK, tm), BF16)],
        compiler_params=_cparams("arbitrary", "arbitrary"),
        name="peer_experts",
    )(h2_t, u, v_t, r2, e2, n1, e1, x, gate)


def _rope_tables(pos, half):
    freqs = ROPE_THETA ** (-jnp.arange(half, dtype=F32) / half)
    ang = pos.astype(F32)[:, None] * freqs[None, :]
    cos, sin = jnp.cos(ang), jnp.sin(ang)
    return jnp.concatenate([cos, cos], axis=1), jnp.concatenate([-sin, sin], axis=1)


def _channel_mixer(x_mid, shift2, scale2, gate2, rows_per_group, norm2_g, w_peer_q, keys_flat, peer_u, peer_vt,
                   n_peer_heads, tm):
    h2, h2_t = norm_modulate(x_mid, norm2_g, scale2, shift2, rows_per_group, tm, with_transpose=True)
    (pq_hm,) = project(h2, w_peer_q, 0, w_peer_q.shape[1], outs=(("head_major", BF16),))
    r2, e2, n1, e1 = peer_route(pq_hm, keys_flat, n_peer_heads)
    return peer_experts(h2_t, peer_u, peer_vt, r2, e2, n1, e1, x_mid, gate2, rows_per_group, tm=tm)


def kernel(x_prompt, x_sample, cache_k, cache_v, cache_idx_k, state_pool, page_table, c_prompt, c_sample,
           norm1_g, norm2_g, w_mod, b_mod, w_in, q_norm_g, k_norm_g, w_attn_proj, w_pool, pool_scale,
           w_out, w_peer_q, peer_keys, peer_u, peer_v):
    b, s, d = x_prompt.shape
    bd, t_new, _ = x_sample.shape
    hd = q_norm_g.shape[0]
    idx_dim = cache_idx_k.shape[2]
    page = cache_k.shape[1]
    n_kv = cache_k.shape[2]
    n_heads = w_attn_proj.shape[0] // hd
    d_pool = state_pool.shape[2]
    n_state = state_pool.shape[1]
    q_cols, kv_cols = n_heads * hd, n_kv * hd
    idx_heads = (w_in.shape[1] - q_cols - 2 * kv_cols - idx_dim - d_pool - 2 * d) // (idx_dim + 1)
    iq_cols = idx_heads * idx_dim
    past = page_table.shape[1] * page
    n_peer_heads, _, nk, half = peer_keys.shape
    assert hd == LANES and idx_dim == LANES and nk == LANES and half == LANES

    o_ik = q_cols + 2 * kv_cols + iq_cols
    o_iw = o_ik + idx_dim
    iw_pad = jnp.zeros((d, LANES - idx_heads), BF16)
    w_in_c = w_in.astype(BF16)
    w_in_b = jnp.concatenate([w_in_c[:, :o_ik], w_in_c[:, o_iw + idx_heads:], w_in_c[:, o_ik:o_iw],
                              w_in_c[:, o_iw:o_iw + idx_heads], iw_pad], axis=1)
    c_q, c_k, c_v = 0, q_cols, q_cols + kv_cols
    c_iq = c_v + kv_cols
    c_pool = c_iq + iq_cols
    c_ga = c_pool + d_pool
    c_gb = c_ga + d
    c_ik = c_gb + d
    c_iw = c_ik + idx_dim
    w_ap_b = w_attn_proj.astype(BF16)
    w_out_b = w_out.astype(BF16)
    w_pq_b = w_peer_q.astype(BF16)
    w_pool_b = w_pool.astype(BF16)
    keys_flat = peer_keys.reshape(n_peer_heads * 2, nk, half).astype(BF16)
    peer_u_b = peer_u.astype(BF16)
    peer_vt_b = peer_v.astype(BF16).T
    iw_scale = idx_heads ** -0.5 * idx_dim ** -0.5

    mod = modulation(jnp.concatenate([c_prompt, c_sample], axis=0), w_mod, b_mod)
    mod = mod.reshape(b + bd, 6, d)
    mod_p = [mod[:b, i][:, None, :] for i in range(6)]
    mod_s = [jnp.repeat(mod[b:, i], t_new, axis=0) for i in range(6)]

    def mixer_inputs(x2, shift, scale, pos, rows_per_group, tm):
        cos, sin = (jnp.tile(tbl, (x2.shape[0] // pos.shape[0], 1)) for tbl in _rope_tables(pos, hd // 2))
        (h,) = norm_modulate(x2, norm1_g, scale, shift, rows_per_group, tm)
        (q_hm,) = project(h, w_in_b, c_q, q_cols, outs=(("head_major", BF16),), norm_g=q_norm_g, cos=cos, sin=sin, tm=tm)
        k_f, k_b = project(h, w_in_b, c_k, kv_cols, outs=(("token_major", F32), ("token_major", BF16)),
                           norm_g=k_norm_g, cos=cos, sin=sin, tm=tm)
        v_f, v_b = project(h, w_in_b, c_v, kv_cols, outs=(("token_major", F32), ("token_major", BF16)), tm=tm)
        (iq_hm,) = project(h, w_in_b, c_iq, iq_cols, outs=(("head_major", BF16),), cos=cos, sin=sin, tm=tm)
        ik_f, ik_b = project(h, w_in_b, c_ik, idx_dim, outs=(("token_major", F32), ("token_major", BF16)),
                             cos=cos, sin=sin, tm=tm)
        (iw,) = project(h, w_in_b, c_iw, LANES, outs=(("token_major", F32),), scale=iw_scale, tm=tm)
        (p_in,) = project(h, w_in_b, c_pool, d_pool, outs=(("token_major", F32),), tm=tm)
        (sig_a,) = project(h, w_in_b, c_ga, d, outs=(("token_major", BF16),), act="sigmoid", tm=tm)
        (sig_b,) = project(h, w_in_b, c_gb, d, outs=(("token_major", BF16),), act="sigmoid", tm=tm)
        return q_hm, k_f, k_b, v_f, v_b, iq_hm, ik_f, ik_b, iw[:, :idx_heads], p_in, sig_a, sig_b

    xp = x_prompt.reshape(b * s, d)
    tm_p = min(1024, s)
    pos_p = jnp.arange(s)
    (q_hm, k_f, k_b, v_f, v_b, iq_hm, ik_f, ik_b, iw, p_in, sig_a, sig_b) = mixer_inputs(
        xp, mod_p[0], mod_p[1], pos_p, s, tm_p)
    attn_p = prompt_attention(iq_hm, iw.T, ik_b, q_hm, k_b, v_b, b, s, min(TOPK_MAX, s // 4))
    pool_p = pool_prompt(p_in.reshape(b, s, d_pool), w_pool_b, pool_scale)
    merged = merge_branches(attn_p, w_ap_b, sig_a, sig_b, pool_p)
    h_p = out_project(merged, w_out_b, xp, mod_p[2], s)
    y_prompt = _channel_mixer(h_p, mod_p[3], mod_p[4], mod_p[5], s, norm2_g, w_pq_b, keys_flat, peer_u_b,
                              peer_vt_b, n_peer_heads, min(512, s))
    k_prompt = k_f.reshape(b, s, n_kv, hd)
    v_prompt = v_f.reshape(b, s, n_kv, hd)
    idx_k_prompt = ik_f.reshape(b, s, idx_dim)
    pool_state_prompt = p_in.reshape(b, s, d_pool)[:, s - n_state:, :]

    ns = bd * t_new
    xs = x_sample.reshape(ns, d)
    pos_s = past + jnp.arange(t_new)
    (q_hm, k_f, k_b, v_f, v_b, iq_hm, ik_f, ik_b, iw, p_in, sig_a, sig_b) = mixer_inputs(
        xs, mod_s[0], mod_s[1], pos_s, t_new, ns)
    group = n_heads // n_kv
    iq_rows = iq_hm.reshape(idx_heads, bd, t_new, idx_dim).transpose(1, 0, 2, 3).reshape(bd, idx_heads * t_new, idx_dim)
    iq_rows = jnp.pad(iq_rows, ((0, 0), (0, LANES - idx_heads * t_new), (0, 0)))
    iw_row = iw.reshape(bd, t_new, idx_heads).transpose(0, 2, 1).reshape(bd, 1, idx_heads * t_new)
    iw_row = jnp.pad(iw_row, ((0, 0), (0, 0), (0, LANES - idx_heads * t_new)))
    q5 = q_hm.reshape(n_kv, group, bd, t_new, hd).transpose(2, 0, 1, 3, 4)
    eye = jnp.eye(n_kv, dtype=q5.dtype)
    q_blk = (q5[:, :, :, :, None, :] * eye[None, :, None, None, :, None]).reshape(
        bd, n_kv * group * t_new, n_kv * hd)
    q_blk = jnp.pad(q_blk, ((0, 0), (0, LANES - n_kv * group * t_new), (0, 0)))
    new_rows = 16
    def pad_new(a, *tail):
        a = a.reshape((bd, t_new) + tail)
        return jnp.pad(a, ((0, 0), (0, new_rows - t_new)) + ((0, 0),) * len(tail))

    att = sample_attention(iq_rows, iw_row, q_blk, pad_new(ik_f, idx_dim), pad_new(k_f, n_kv, hd),
                           pad_new(v_f, n_kv, hd), cache_idx_k, cache_k, cache_v,
                           page_table, t_new, min(TOPK_MAX, (past + t_new) // 4))
    att = att[:, :n_kv * group * t_new].reshape(bd, n_kv, group, t_new, n_kv, hd)
    att = jnp.einsum('bkgqkd->bqkgd', att).reshape(ns, q_cols).astype(BF16)
    pad_t = jnp.concatenate([state_pool, p_in.reshape(bd, t_new, d_pool)], axis=1)
    pool_s = pool_sample(pad_t.transpose(1, 0, 2), w_pool_b, pool_scale, t_new, past)
    pool_s = pool_s.transpose(1, 0, 2).reshape(ns, d)
    merged = merge_branches(att, w_ap_b, sig_a, sig_b, pool_s)
    h_s = out_project(merged, w_out_b, xs, mod_s[2], t_new)
    y_sample = _channel_mixer(h_s, mod_s[3], mod_s[4], mod_s[5], t_new, norm2_g, w_pq_b, keys_flat, peer_u_b,
                              peer_vt_b, n_peer_heads, ns)

    return (y_prompt.reshape(b, s, d), y_sample.reshape(bd, t_new, d), k_prompt, v_prompt, idx_k_prompt,
            pool_state_prompt, k_f.reshape(bd, t_new, n_kv, hd), v_f.reshape(bd, t_new, n_kv, hd),
            ik_f.reshape(bd, t_new, idx_dim), pad_t[:, t_new:, :])
```
